```python
import jax
import jax.numpy as jnp
from jax import lax
import numpy as np

D_MODEL = 1024
BATCH = 2
SEQ = 8192
DEPTH = 2

PLE_DIM = 256
POOL_WINDOWS = (2, 4, 8, 16)
N_POOL_GROUPS = len(POOL_WINDOWS)
POOL_GROUP = D_MODEL // 8
POOL_WIDTH = N_POOL_GROUPS * POOL_GROUP
N_HEADS = 8
Q_LORA = D_MODEL // 2
KV_LORA = D_MODEL // 4
QK_NOPE = 128
QK_ROPE = 64
QK_HEAD = QK_NOPE + QK_ROPE
V_HEAD = 128
ATTN_WIDTH = N_HEADS * V_HEAD
D_FF = -(-8 * D_MODEL // (3 * 256)) * 256
ROPE_THETA = 10000.0
EPS = 1e-6
Q_BLOCK = 128

_OFF_Q = POOL_WIDTH
_OFF_KV = _OFF_Q + Q_LORA
_OFF_KR = _OFF_KV + KV_LORA
_OFF_GA = _OFF_KR + QK_ROPE
_OFF_GB = _OFF_GA + D_MODEL
IN_WIDTH = _OFF_GB + D_MODEL

kernel_name = 'hybrid_pool_mla_gated_block'


def rms_norm(x, g):
    xf = x.astype(jnp.float32)
    y = xf * lax.rsqrt(jnp.mean(xf * xf, axis=-1, keepdims=True) + EPS)
    return (y * g.astype(jnp.float32)).astype(x.dtype)


def rope_tables(positions):
    inv_freq = 1.0 / (ROPE_THETA ** (jnp.arange(0, QK_ROPE, 2, dtype=jnp.float32) / QK_ROPE))
    ang = positions.astype(jnp.float32)[..., None] * inv_freq
    return jnp.cos(ang)[:, :, None, :], jnp.sin(ang)[:, :, None, :]


def apply_rope(x, cos, sin):
    xf = x.astype(jnp.float32)
    half = QK_ROPE // 2
    x1, x2 = xf[..., :half], xf[..., half:]
    return jnp.concatenate([x1 * cos - x2 * sin, x2 * cos + x1 * sin], axis=-1).astype(x.dtype)


def multiscale_pool(u, w_pool, pool_scale):
    B, S, _ = u.shape
    uf = u.astype(jnp.float32).reshape(B, S, N_POOL_GROUPS, POOL_GROUP)
    csum = jnp.concatenate([jnp.zeros((B, 1, N_POOL_GROUPS, POOL_GROUP), jnp.float32),
                            jnp.cumsum(uf, axis=1)], axis=1)
    t = jnp.arange(S)
    pooled = []
    for g, w in enumerate(POOL_WINDOWS):
        hi = csum[:, 1:, g]
        lo = jnp.concatenate([jnp.zeros((B, w - 1, POOL_GROUP), jnp.float32),
                              csum[:, :S - w + 1, g]], axis=1)
        cnt = jnp.minimum(t + 1, w).astype(jnp.float32)[None, :, None]
        pooled.append((hi - lo) / cnt - uf[:, :, g])
    pooled = jnp.stack(pooled, axis=2).astype(u.dtype)
    mixed = jnp.einsum('bsgc,gcd->bsgd', pooled, w_pool)
    return mixed.reshape(B, S, POOL_WIDTH) * pool_scale


def causal_block_attention(q, k, v):
    B, S, H, Dq = q.shape
    nb = S // Q_BLOCK
    scale = QK_HEAD ** -0.5
    q_blocks = q.reshape(B, nb, Q_BLOCK, H, Dq).transpose(1, 0, 2, 3, 4)
    key_idx = jnp.arange(S)
    neg = jnp.finfo(jnp.float32).min

    def one_block(args):
        qb, bi = args
        s = jnp.einsum('bqhd,bkhd->bhqk', qb, k, preferred_element_type=jnp.float32) * scale
        q_idx = bi * Q_BLOCK + jnp.arange(Q_BLOCK)
        mask = key_idx[None, :] <= q_idx[:, None]
        s = jnp.where(mask[None, None], s, neg)
        pr = jax.nn.softmax(s, axis=-1)
        return jnp.einsum('bhqk,bkhd->bqhd', pr.astype(v.dtype), v)

    out = lax.map(one_block, (q_blocks, jnp.arange(nb)))
    return out.transpose(1, 0, 2, 3, 4).reshape(B, S, H * v.shape[-1])


def mla(c_q, c_kv, k_rope, q_norm_g, kv_norm_g, w_uq, w_ukv, cos, sin):
    B, S, _ = c_q.shape
    q = jnp.einsum('bsr,rhd->bshd', rms_norm(c_q, q_norm_g), w_uq)
    q = jnp.concatenate([q[..., :QK_NOPE], apply_rope(q[..., QK_NOPE:], cos, sin)], axis=-1)
    kv = jnp.einsum('bsr,rhd->bshd', rms_norm(c_kv, kv_norm_g), w_ukv)
    k_nope, v = kv[..., :QK_NOPE], kv[..., QK_NOPE:]
    k_pe = apply_rope(k_rope[:, :, None, :], cos, sin)
    k = jnp.concatenate([k_nope, jnp.broadcast_to(k_pe, (B, S, N_HEADS, QK_ROPE))], axis=-1)
    return causal_block_attention(q, k, v)


def setup_inputs(seed: int = 0) -> dict:
    key = jax.random.key(seed)
    ks = jax.random.split(key, 24)
    f32 = jnp.float32

    def dense(k, shape, fan_in):
        return jax.random.normal(k, shape, f32) * fan_in ** -0.5

    def gain(k, shape, s=0.05):
        return 1.0 + s * jax.random.normal(k, shape, f32)

    L = DEPTH
    return {
        'x': jax.random.normal(ks[0], (BATCH, SEQ, D_MODEL), f32),
        'p': jax.random.normal(ks[1], (L, BATCH, SEQ, PLE_DIM), f32),
        'positions': jnp.tile(jnp.arange(SEQ, dtype=jnp.int32)[None, :], (BATCH, 1)),
        'norm_mix': gain(ks[2], (L, D_MODEL)),
        'w_in': dense(ks[3], (L, D_MODEL, IN_WIDTH), D_MODEL),
        'w_pool': dense(ks[4], (L, N_POOL_GROUPS, POOL_GROUP, POOL_GROUP), POOL_GROUP),
        'pool_scale': gain(ks[5], (L, POOL_WIDTH), 0.1),
        'q_norm': gain(ks[6], (L, Q_LORA)),
        'kv_norm': gain(ks[7], (L, KV_LORA)),
        'w_uq': dense(ks[8], (L, Q_LORA, N_HEADS, QK_HEAD), Q_LORA),
        'w_ukv': dense(ks[9], (L, KV_LORA, N_HEADS, QK_NOPE + V_HEAD), KV_LORA),
        'w_a': dense(ks[10], (L, POOL_WIDTH, D_MODEL), POOL_WIDTH),
        'w_b': dense(ks[11], (L, ATTN_WIDTH, D_MODEL), ATTN_WIDTH),
        'w_o': dense(ks[12], (L, D_MODEL, D_MODEL), D_MODEL),
        'norm_ffn': gain(ks[13], (L, D_MODEL)),
        'w_gate': dense(ks[14], (L, D_MODEL, D_FF), D_MODEL),
        'w_up': dense(ks[15], (L, D_MODEL, D_FF), D_MODEL),
        'w_down': dense(ks[16], (L, D_FF, D_MODEL), D_FF),
        'norm_ple': gain(ks[17], (L, D_MODEL)),
        'w_ple_gate': dense(ks[18], (L, D_MODEL, D_MODEL), D_MODEL),
        'w_ple': dense(ks[19], (L, PLE_DIM, D_MODEL), PLE_DIM),
        'final_norm': gain(ks[20], (D_MODEL,)),
    }


def reference(x, p, positions, norm_mix, w_in, w_pool, pool_scale, q_norm, kv_norm, w_uq, w_ukv,
              w_a, w_b, w_o, norm_ffn, w_gate, w_up, w_down, norm_ple, w_ple_gate, w_ple,
              final_norm):
    cos, sin = rope_tables(positions)
    for i in range(DEPTH):
        h = rms_norm(x, norm_mix[i])
        z = h @ w_in[i]
        u = z[..., :_OFF_Q]
        c_q = z[..., _OFF_Q:_OFF_KV]
        c_kv = z[..., _OFF_KV:_OFF_KR]
        k_rope = z[..., _OFF_KR:_OFF_GA]
        gate_a = z[..., _OFF_GA:_OFF_GB]
        gate_b = z[..., _OFF_GB:]
        y_a = multiscale_pool(u, w_pool[i], pool_scale[i]) @ w_a[i]
        y_b = mla(c_q, c_kv, k_rope, q_norm[i], kv_norm[i], w_uq[i], w_ukv[i], cos, sin) @ w_b[i]
        merged = jax.nn.sigmoid(gate_a) * y_a + jax.nn.sigmoid(gate_b) * y_b
        x = x + merged @ w_o[i]
        h = rms_norm(x, norm_ffn[i])
        x = x + (jax.nn.silu(h @ w_gate[i]) * (h @ w_up[i])) @ w_down[i]
        g = jax.nn.sigmoid(rms_norm(x, norm_ple[i]) @ w_ple_gate[i])
        x = x + g * (p[i] @ w_ple[i])
    return rms_norm(x, final_norm)
```

```python
import functools

import jax
import jax.numpy as jnp
from jax import lax
from jax.experimental import pallas as pl
from jax.experimental.pallas import tpu as pltpu

D_MODEL = 1024
PLE_DIM = 256
POOL_WINDOWS = (2, 4, 8, 16)
POOL_GROUP = 128
POOL_WIDTH = 512
N_HEADS = 8
Q_LORA = 512
KV_LORA = 256
QK_NOPE = 128
QK_ROPE = 64
QK_HEAD = QK_NOPE + QK_ROPE
V_HEAD = 128
D_FF = 2816
ROPE_THETA = 10000.0
EPS = 1e-6

LANES = 128
QK_PAD = 2 * LANES
POOL_HALO = 16

OFF_U = 0
OFF_CQ = OFF_U + POOL_WIDTH
OFF_CKV = OFF_CQ + Q_LORA
OFF_KR = OFF_CKV + KV_LORA
OFF_GA = OFF_KR + LANES
OFF_GB = OFF_GA + D_MODEL
IN_WIDTH_PAD = OFF_GB + D_MODEL

ROW_TILE = 512
Q_TILE = 512
KV_TILE = 512
FF_CHUNK = 256
VMEM_LIMIT = 56 * 1024 * 1024

MASK_VALUE = -1e30
LOG2E = 1.4426950408889634

_f32 = jnp.float32
_bf16 = jnp.bfloat16


def _dot(a, b):
    return jnp.dot(a, b, preferred_element_type=_f32)


def _rms(x, g):
    return x * lax.rsqrt(jnp.mean(x * x, axis=-1, keepdims=True) + EPS) * g


def _resident(shape):
    nd = len(shape)
    return pl.BlockSpec(shape, lambda *_: (0,) * nd, pipeline_mode=pl.Buffered(1))


def _params(sem):
    return pltpu.CompilerParams(dimension_semantics=sem, vmem_limit_bytes=VMEM_LIMIT)


def _rope_table_kernel(pos_ref, inv_ref, msk_ref, cos_ref, sin_ref):
    ang = pos_ref[...] * inv_ref[...]
    cos_ref[...] = jnp.cos(ang) * msk_ref[...]
    sin_ref[...] = jnp.sin(ang) * msk_ref[...]


def _rope_tables(pos_col, inv_row, msk_row, tm):
    t = pos_col.shape[0]
    row = pl.BlockSpec((tm, LANES), lambda i: (i, 0))
    return pl.pallas_call(
        _rope_table_kernel,
        grid=(t // tm,),
        in_specs=[pl.BlockSpec((tm, 1), lambda i: (i, 0)), _resident((1, LANES)), _resident((1, LANES))],
        out_specs=[row, row],
        out_shape=[jax.ShapeDtypeStruct((t, LANES), _f32)] * 2,
        compiler_params=_params(("arbitrary",)),
        name="rope_tables",
    )(pos_col, inv_row, msk_row)


def _front_kernel(x_ref, g_ref, w_in_ref, w_pool_ref, ps_ref, qg_ref, kvg_ref, w_uq_ref, w_ukv_ref,
                  cos_ref, sin_ref,
                  mixed_ref, q_ref, k_ref, v_ref, sga_ref, sgb_ref,
                  carry_ref, *, tm, seq):
    i = pl.program_id(0)
    t0 = (i * tm) % seq
    h = _rms(x_ref[...], g_ref[...]).astype(_bf16)

    u = _dot(h, w_in_ref[:, OFF_U:OFF_U + POOL_WIDTH])

    @pl.when(t0 == 0)
    def _():
        carry_ref[...] = jnp.zeros_like(carry_ref)

    ext = jnp.concatenate([carry_ref[...], u], axis=0)
    carry_ref[...] = u[tm - POOL_HALO:, :]
    sums = []
    cur = ext
    step = 1
    for _ in POOL_WINDOWS:
        cur = cur + pltpu.roll(cur, step, axis=0)
        sums.append(cur[POOL_HALO:, :POOL_GROUP])
        cur = cur[:, POOL_GROUP:]
        step *= 2
    t_idx = t0 + lax.broadcasted_iota(jnp.int32, (tm, POOL_GROUP), 0)
    for g, w in enumerate(POOL_WINDOWS):
        cols = slice(g * POOL_GROUP, (g + 1) * POOL_GROUP)
        cnt = jnp.minimum(t_idx + 1, w).astype(_f32)
        pooled = sums[g] / cnt - u[:, cols]
        mixed = _dot(pooled.astype(_bf16), w_pool_ref[g]) * ps_ref[:, cols]
        mixed_ref[:, cols] = mixed.astype(_bf16)

    cos_t = cos_ref[...]
    sin_t = sin_ref[...]

    def rope(v):
        return v * cos_t + pltpu.roll(v, QK_ROPE, axis=1) * sin_t

    cq = _dot(h, w_in_ref[:, OFF_CQ:OFF_CQ + Q_LORA])
    cqn = _rms(cq, qg_ref[...]).astype(_bf16)
    for hd in range(N_HEADS):
        qh = _dot(cqn, w_uq_ref[:, hd * QK_PAD:(hd + 1) * QK_PAD])
        q_ref[:, hd * QK_PAD:hd * QK_PAD + LANES] = qh[:, :LANES].astype(_bf16)
        q_ref[:, hd * QK_PAD + LANES:(hd + 1) * QK_PAD] = rope(qh[:, LANES:]).astype(_bf16)

    ckv = _dot(h, w_in_ref[:, OFF_CKV:OFF_CKV + KV_LORA])
    ckvn = _rms(ckv, kvg_ref[...]).astype(_bf16)
    k_pe = rope(_dot(h, w_in_ref[:, OFF_KR:OFF_KR + LANES])).astype(_bf16)
    for hd in range(N_HEADS):
        kh = _dot(ckvn, w_ukv_ref[:, hd * LANES:(hd + 1) * LANES])
        k_ref[:, hd * QK_PAD:hd * QK_PAD + LANES] = kh.astype(_bf16)
        k_ref[:, hd * QK_PAD + LANES:(hd + 1) * QK_PAD] = k_pe
    v_off = N_HEADS * QK_NOPE
    v_ref[...] = _dot(ckvn, w_ukv_ref[:, v_off:v_off + N_HEADS * V_HEAD]).astype(_bf16)

    sga_ref[...] = jax.nn.sigmoid(_dot(h, w_in_ref[:, OFF_GA:OFF_GA + D_MODEL])).astype(_bf16)
    sgb_ref[...] = jax.nn.sigmoid(_dot(h, w_in_ref[:, OFF_GB:OFF_GB + D_MODEL])).astype(_bf16)


def _front(x2, g, w_in, w_pool, ps, qg, kvg, w_uq, w_ukv, cos_t, sin_t, seq):
    t = x2.shape[0]
    tm = ROW_TILE

    def row(width):
        return pl.BlockSpec((tm, width), lambda i: (i, 0))

    out_widths = (POOL_WIDTH, N_HEADS * QK_PAD, N_HEADS * QK_PAD, N_HEADS * V_HEAD, D_MODEL, D_MODEL)
    return pl.pallas_call(
        functools.partial(_front_kernel, tm=tm, seq=seq),
        grid=(t // tm,),
        in_specs=[row(D_MODEL), _resident(g.shape), _resident(w_in.shape), _resident(w_pool.shape),
                  _resident(ps.shape), _resident(qg.shape), _resident(kvg.shape), _resident(w_uq.shape),
                  _resident(w_ukv.shape), row(LANES), row(LANES)],
        out_specs=[row(w) for w in out_widths],
        out_shape=[jax.ShapeDtypeStruct((t, w), _bf16) for w in out_widths],
        scratch_shapes=[pltpu.VMEM((POOL_HALO, POOL_WIDTH), _f32)],
        compiler_params=_params(("arbitrary",)),
        name="front",
    )(x2, g, w_in, w_pool, ps, qg, kvg, w_uq, w_ukv, cos_t, sin_t)


def _attn_kernel(q_ref, k_ref, v_ref, o_ref, *, tq, tk):
    i = pl.program_id(2)
    q = q_ref[...]
    n_full = (i * tq) // tk

    def step(j, carry, masked):
        m, l, acc = carry
        start = pl.multiple_of(j * tk, tk)
        k = k_ref[pl.ds(start, tk), :]
        v = v_ref[pl.ds(start, tk), :]
        s = lax.dot_general(q, k, (((1,), (1,)), ((), ())), preferred_element_type=_f32)
        if masked:
            row = i * tq + lax.broadcasted_iota(jnp.int32, (tq, tk), 0)
            col = j * tk + lax.broadcasted_iota(jnp.int32, (tq, tk), 1)
            s = jnp.where(col <= row, s, MASK_VALUE)
        m_new = jnp.maximum(m, jnp.max(s, axis=1, keepdims=True))
        alpha = jnp.exp2(m - m_new)
        p = jnp.exp2(s - m_new)
        l = alpha * l + jnp.sum(p, axis=1, keepdims=True)
        acc = alpha * acc + _dot(p.astype(_bf16), v)
        return m_new, l, acc

    init = (jnp.full((tq, 1), MASK_VALUE, _f32), jnp.zeros((tq, 1), _f32), jnp.zeros((tq, V_HEAD), _f32))
    carry = lax.fori_loop(0, n_full, lambda j, c: step(j, c, False), init)
    _, l, acc = step(n_full, carry, True)
    o_ref[...] = (acc / l).astype(_bf16)


def _attention(q, k, v, batch, seq):
    tq, tk = Q_TILE, KV_TILE
    nq = seq // tq
    return pl.pallas_call(
        functools.partial(_attn_kernel, tq=tq, tk=tk),
        grid=(batch, N_HEADS, nq),
        in_specs=[pl.BlockSpec((tq, QK_PAD), lambda b, h, i: (b * nq + i, h)),
                  pl.BlockSpec((seq, QK_PAD), lambda b, h, i: (b, h)),
                  pl.BlockSpec((seq, V_HEAD), lambda b, h, i: (b, h))],
        out_specs=pl.BlockSpec((tq, V_HEAD), lambda b, h, i: (b * nq + i, h)),
        out_shape=jax.ShapeDtypeStruct((batch * seq, N_HEADS * V_HEAD), _bf16),
        compiler_params=_params(("arbitrary", "arbitrary", "arbitrary")),
        name="attention",
    )(q, k, v)


def _merge_kernel(x_ref, attn_ref, mixed_ref, sga_ref, sgb_ref, w_a_ref, w_b_ref, w_o_ref, o_ref):
    y_a = _dot(mixed_ref[...], w_a_ref[...])
    y_b = _dot(attn_ref[...], w_b_ref[...])
    merged = sga_ref[...].astype(_f32) * y_a + sgb_ref[...].astype(_f32) * y_b
    o_ref[...] = x_ref[...] + _dot(merged.astype(_bf16), w_o_ref[...])


def _merge(x2, attn, mixed, sga, sgb, w_a, w_b, w_o):
    t = x2.shape[0]
    tm = ROW_TILE

    def row(width):
        return pl.BlockSpec((tm, width), lambda i: (i, 0))

    return pl.pallas_call(
        _merge_kernel,
        grid=(t // tm,),
        in_specs=[row(D_MODEL), row(D_MODEL), row(POOL_WIDTH), row(D_MODEL), row(D_MODEL),
                  _resident(w_a.shape), _resident(w_b.shape), _resident(w_o.shape)],
        out_specs=row(D_MODEL),
        out_shape=jax.ShapeDtypeStruct((t, D_MODEL), _f32),
        compiler_params=_params(("arbitrary",)),
        name="merge",
    )(x2, attn, mixed, sga, sgb, w_a, w_b, w_o)


def _ffn_kernel(x_ref, p_ref, gf_ref, w_gate_ref, w_up_ref, w_down_ref, gp_ref, w_pg_ref, w_ple_ref, gl_ref,
                o_ref, *, final):
    x = x_ref[...]
    h = _rms(x, gf_ref[...]).astype(_bf16)
    acc = None
    for c in range(D_FF // FF_CHUNK):
        cols = slice(c * FF_CHUNK, (c + 1) * FF_CHUNK)
        gate = _dot(h, w_gate_ref[:, cols])
        up = _dot(h, w_up_ref[:, cols])
        act = (gate * jax.nn.sigmoid(gate) * up).astype(_bf16)
        part = _dot(act, w_down_ref[cols, :])
        acc = part if acc is None else acc + part
    x = x + acc
    hp = _rms(x, gp_ref[...]).astype(_bf16)
    gate = jax.nn.sigmoid(_dot(hp, w_pg_ref[...]))
    x = x + gate * _dot(p_ref[...].astype(_bf16), w_ple_ref[...])
    if final:
        x = _rms(x, gl_ref[...])
    o_ref[...] = x


def _ffn(x2, p2, gf, w_gate, w_up, w_down, gp, w_pg, w_ple, gl, final):
    t = x2.shape[0]
    tm = ROW_TILE

    def row(width):
        return pl.BlockSpec((tm, width), lambda i: (i, 0))

    return pl.pallas_call(
        functools.partial(_ffn_kernel, final=final),
        grid=(t // tm,),
        in_specs=[row(D_MODEL), row(PLE_DIM), _resident(gf.shape), _resident(w_gate.shape),
                  _resident(w_up.shape), _resident(w_down.shape), _resident(gp.shape),
                  _resident(w_pg.shape), _resident(w_ple.shape), _resident(gl.shape)],
        out_specs=row(D_MODEL),
        out_shape=jax.ShapeDtypeStruct((t, D_MODEL), _f32),
        compiler_params=_params(("arbitrary",)),
        name="ffn_ple",
    )(x2, p2, gf, w_gate, w_up, w_down, gp, w_pg, w_ple, gl)


def _rotated(w):
    half = QK_ROPE // 2
    return jnp.concatenate([-w[..., half:], w[..., :half]], axis=-1)


def _prep_w_in(w_in):
    kr = w_in[:, OFF_KR:OFF_KR + QK_ROPE]
    return jnp.concatenate([w_in[:, :OFF_KR], kr, _rotated(kr), w_in[:, OFF_KR + QK_ROPE:]], axis=1).astype(_bf16)


def _prep_w_uq(w_uq):
    scale = QK_HEAD ** -0.5 * LOG2E
    pe = w_uq[..., QK_NOPE:]
    w = jnp.concatenate([w_uq[..., :QK_NOPE], pe, _rotated(pe)], axis=-1) * scale
    return w.reshape(Q_LORA, N_HEADS * QK_PAD).astype(_bf16)


def _prep_w_ukv(w_ukv):
    k_part = w_ukv[..., :QK_NOPE].reshape(KV_LORA, N_HEADS * QK_NOPE)
    v_part = w_ukv[..., QK_NOPE:].reshape(KV_LORA, N_HEADS * V_HEAD)
    return jnp.concatenate([k_part, v_part], axis=1).astype(_bf16)


def kernel(x, p, positions, norm_mix, w_in, w_pool, pool_scale, q_norm, kv_norm, w_uq, w_ukv, w_a, w_b, w_o,
           norm_ffn, w_gate, w_up, w_down, norm_ple, w_ple_gate, w_ple, final_norm):
    batch, seq, _ = x.shape
    depth = w_in.shape[0]
    t = batch * seq
    assert seq % ROW_TILE == 0 and seq % Q_TILE == 0 and KV_TILE % Q_TILE == 0 and seq % KV_TILE == 0

    inv_freq = 1.0 / (ROPE_THETA ** (jnp.arange(0, QK_ROPE, 2, dtype=_f32) / QK_ROPE))
    zeros = jnp.zeros((QK_ROPE,), _f32)
    inv_row = jnp.concatenate([inv_freq, inv_freq, zeros])[None, :]
    msk_row = jnp.concatenate([jnp.ones((QK_ROPE,), _f32), zeros])[None, :]
    pos_col = positions.astype(_f32).reshape(t, 1)
    cos_t, sin_t = _rope_tables(pos_col, inv_row, msk_row, 2048)

    def vec(a):
        return a.reshape(1, -1)

    x2 = x.reshape(t, D_MODEL)
    for i in range(depth):
        mixed, q, k, v, sga, sgb = _front(
            x2, vec(norm_mix[i]), _prep_w_in(w_in[i]), w_pool[i].astype(_bf16), vec(pool_scale[i]),
            vec(q_norm[i]), vec(kv_norm[i]), _prep_w_uq(w_uq[i]), _prep_w_ukv(w_ukv[i]), cos_t, sin_t, seq)
        attn = _attention(q, k, v, batch, seq)
        x2 = _merge(x2, attn, mixed, sga, sgb, w_a[i].astype(_bf16), w_b[i].astype(_bf16), w_o[i].astype(_bf16))
        x2 = _ffn(x2, p[i].reshape(t, PLE_DIM), vec(norm_ffn[i]), w_gate[i].astype(_bf16), w_up[i].astype(_bf16),
                  w_down[i].astype(_bf16), vec(norm_ple[i]), w_ple_gate[i].astype(_bf16), w_ple[i].astype(_bf16),
                  vec(final_norm), final=(i == depth - 1))
    return x2.reshape(batch, seq, D_MODEL)
```

```python
import functools

import jax
import jax.numpy as jnp
from jax import lax
from jax.experimental import pallas as pl
from jax.experimental.pallas import tpu as pltpu

D_MODEL = 1024
PLE_DIM = 256
POOL_WINDOWS = (2, 4, 8, 16)
POOL_GROUP = 128
POOL_WIDTH = 512
N_HEADS = 8
Q_LORA = 512
KV_LORA = 256
QK_NOPE = 128
QK_ROPE = 64
QK_HEAD = QK_NOPE + QK_ROPE
V_HEAD = 128
D_FF = 2816
ROPE_THETA = 10000.0
EPS = 1e-6

LANES = 128
QK_PAD = 2 * LANES
POOL_HALO = 16

OFF_U = 0
OFF_CQ = OFF_U + POOL_WIDTH
OFF_CKV = OFF_CQ + Q_LORA
OFF_KR = OFF_CKV + KV_LORA
OFF_GA = OFF_KR + LANES
OFF_GB = OFF_GA + D_MODEL
IN_WIDTH_PAD = OFF_GB + D_MODEL

ROW_TILE = 512
ATTN_TILE = 1024
ATTN_CHUNK = 128
FF_CHUNK = 256
VMEM_LIMIT = 56 * 1024 * 1024

MASK_VALUE = -1e30
LOG2E = 1.4426950408889634

_f32 = jnp.float32
_bf16 = jnp.bfloat16


def _dot(a, b):
    return jnp.dot(a, b, preferred_element_type=_f32)


def _rms(x, g):
    return x * lax.rsqrt(jnp.mean(x * x, axis=-1, keepdims=True) + EPS) * g


def _resident(shape):
    nd = len(shape)
    return pl.BlockSpec(shape, lambda *_: (0,) * nd, pipeline_mode=pl.Buffered(1))


def _params(sem):
    return pltpu.CompilerParams(dimension_semantics=sem, vmem_limit_bytes=VMEM_LIMIT)


def _rope_table_kernel(pos_ref, inv_ref, msk_ref, cos_ref, sin_ref):
    ang = pos_ref[...] * inv_ref[...]
    cos_ref[...] = jnp.cos(ang) * msk_ref[...]
    sin_ref[...] = jnp.sin(ang) * msk_ref[...]


def _rope_tables(pos_col, inv_row, msk_row, tm):
    t = pos_col.shape[0]
    row = pl.BlockSpec((tm, LANES), lambda i: (i, 0))
    return pl.pallas_call(
        _rope_table_kernel,
        grid=(t // tm,),
        in_specs=[pl.BlockSpec((tm, 1), lambda i: (i, 0)), _resident((1, LANES)), _resident((1, LANES))],
        out_specs=[row, row],
        out_shape=[jax.ShapeDtypeStruct((t, LANES), _f32)] * 2,
        compiler_params=_params(("arbitrary",)),
        name="rope_tables",
    )(pos_col, inv_row, msk_row)


def _front_kernel(x_ref, g_ref, w_in_ref, w_pool_ref, ps_ref, qg_ref, kvg_ref, w_uq_ref, w_ukv_ref,
                  cos_ref, sin_ref,
                  mixed_ref, q_ref, k_ref, v_ref, sga_ref, sgb_ref,
                  carry_ref, *, tm, seq):
    i = pl.program_id(0)
    t0 = (i * tm) % seq
    h = _rms(x_ref[...], g_ref[...]).astype(_bf16)

    u = _dot(h, w_in_ref[:, OFF_U:OFF_U + POOL_WIDTH])

    @pl.when(t0 == 0)
    def _():
        carry_ref[...] = jnp.zeros_like(carry_ref)

    ext = jnp.concatenate([carry_ref[...], u], axis=0)
    carry_ref[...] = u[tm - POOL_HALO:, :]
    sums = []
    cur = ext
    step = 1
    for _ in POOL_WINDOWS:
        cur = cur + pltpu.roll(cur, step, axis=0)
        sums.append(cur[POOL_HALO:, :POOL_GROUP])
        cur = cur[:, POOL_GROUP:]
        step *= 2
    t_idx = t0 + lax.broadcasted_iota(jnp.int32, (tm, POOL_GROUP), 0)
    for g, w in enumerate(POOL_WINDOWS):
        cols = slice(g * POOL_GROUP, (g + 1) * POOL_GROUP)
        cnt = jnp.minimum(t_idx + 1, w).astype(_f32)
        pooled = sums[g] / cnt - u[:, cols]
        mixed = _dot(pooled.astype(_bf16), w_pool_ref[g]) * ps_ref[:, cols]
        mixed_ref[:, cols] = mixed.astype(_bf16)

    cos_t = cos_ref[...]
    sin_t = sin_ref[...]

    def rope(v):
        return v * cos_t + pltpu.roll(v, QK_ROPE, axis=1) * sin_t

    cq = _dot(h, w_in_ref[:, OFF_CQ:OFF_CQ + Q_LORA])
    cqn = _rms(cq, qg_ref[...]).astype(_bf16)
    for hd in range(N_HEADS):
        qh = _dot(cqn, w_uq_ref[:, hd * QK_PAD:(hd + 1) * QK_PAD])
        q_ref[:, hd * QK_PAD:hd * QK_PAD + LANES] = qh[:, :LANES].astype(_bf16)
        q_ref[:, hd * QK_PAD + LANES:(hd + 1) * QK_PAD] = rope(qh[:, LANES:]).astype(_bf16)

    ckv = _dot(h, w_in_ref[:, OFF_CKV:OFF_CKV + KV_LORA])
    ckvn = _rms(ckv, kvg_ref[...]).astype(_bf16)
    k_pe = rope(_dot(h, w_in_ref[:, OFF_KR:OFF_KR + LANES])).astype(_bf16)
    for hd in range(N_HEADS):
        kh = _dot(ckvn, w_ukv_ref[:, hd * LANES:(hd + 1) * LANES])
        k_ref[:, hd * QK_PAD:hd * QK_PAD + LANES] = kh.astype(_bf16)
        k_ref[:, hd * QK_PAD + LANES:(hd + 1) * QK_PAD] = k_pe
    v_off = N_HEADS * QK_NOPE
    v_ref[...] = _dot(ckvn, w_ukv_ref[:, v_off:v_off + N_HEADS * V_HEAD]).astype(_bf16)

    sga_ref[...] = jax.nn.sigmoid(_dot(h, w_in_ref[:, OFF_GA:OFF_GA + D_MODEL])).astype(_bf16)
    sgb_ref[...] = jax.nn.sigmoid(_dot(h, w_in_ref[:, OFF_GB:OFF_GB + D_MODEL])).astype(_bf16)


def _front(x2, g, w_in, w_pool, ps, qg, kvg, w_uq, w_ukv, cos_t, sin_t, seq):
    t = x2.shape[0]
    tm = ROW_TILE

    def row(width):
        return pl.BlockSpec((tm, width), lambda i: (i, 0))

    out_widths = (POOL_WIDTH, N_HEADS * QK_PAD, N_HEADS * QK_PAD, N_HEADS * V_HEAD, D_MODEL, D_MODEL)
    return pl.pallas_call(
        functools.partial(_front_kernel, tm=tm, seq=seq),
        grid=(t // tm,),
        in_specs=[row(D_MODEL), _resident(g.shape), _resident(w_in.shape), _resident(w_pool.shape),
                  _resident(ps.shape), _resident(qg.shape), _resident(kvg.shape), _resident(w_uq.shape),
                  _resident(w_ukv.shape), row(LANES), row(LANES)],
        out_specs=[row(w) for w in out_widths],
        out_shape=[jax.ShapeDtypeStruct((t, w), _bf16) for w in out_widths],
        scratch_shapes=[pltpu.VMEM((POOL_HALO, POOL_WIDTH), _f32)],
        compiler_params=_params(("arbitrary",)),
        name="front",
    )(x2, g, w_in, w_pool, ps, qg, kvg, w_uq, w_ukv, cos_t, sin_t)


def _attn_kernel(q_ref, k_ref, v_ref, o_ref, s0_scr, s1_scr, m_scr, l_scr, acc_scr, *, tile, chunk):
    i = pl.program_id(2)
    half = tile // 2
    nt = (((1,), (1,)), ((), ()))

    def scores(rows, blk):
        start = pl.multiple_of(blk * half, half)
        return lax.dot_general(q_ref[rows, :], k_ref[pl.ds(start, half), :], nt, preferred_element_type=_f32)

    def values(blk):
        return v_ref[pl.ds(pl.multiple_of(blk * half, half), half), :]

    def lanes(a, width):
        return jnp.concatenate([a] * (width // LANES), axis=1)

    def absorb(s_ref, s_row0, row0, nrows, v, diag_col0=None):
        for c in range(nrows // chunk):
            rows = pl.ds(row0 + c * chunk, chunk)
            s = s_ref[pl.ds(s_row0 + c * chunk, chunk), :]
            if diag_col0 is not None:
                row_id = row0 + c * chunk + lax.broadcasted_iota(jnp.int32, s.shape, 0)
                col_id = diag_col0 + lax.broadcasted_iota(jnp.int32, s.shape, 1)
                s = jnp.where(col_id <= row_id, s, MASK_VALUE)
            m_old = m_scr[rows, :]
            m_new = jnp.maximum(m_old, jnp.max(s, axis=1, keepdims=True))
            alpha = jnp.exp2(m_old - m_new)
            p = jnp.exp2(s - lanes(m_new, half))
            l_scr[rows, :] = alpha * l_scr[rows, :] + jnp.sum(p, axis=1, keepdims=True)
            m_scr[rows, :] = m_new
            acc_scr[rows, :] = alpha * acc_scr[rows, :] + _dot(p.astype(_bf16), v)

    def finish(row0, nrows):
        rows = pl.ds(row0, nrows)
        o_ref[rows, :] = (acc_scr[rows, :] / l_scr[rows, :]).astype(_bf16)

    m_scr[...] = jnp.full(m_scr.shape, MASK_VALUE, _f32)
    l_scr[...] = jnp.zeros(l_scr.shape, _f32)
    acc_scr[...] = jnp.zeros(acc_scr.shape, _f32)
    full = pl.ds(0, tile)
    s0_scr[...] = scores(full, 0)

    def body(t, carry):
        s1_scr[...] = scores(full, 2 * t + 1)
        absorb(s0_scr, 0, 0, tile, values(2 * t))
        s0_scr[...] = scores(full, 2 * t + 2)
        absorb(s1_scr, 0, 0, tile, values(2 * t + 1))
        return carry

    lax.fori_loop(0, i, body, 0)

    s1_scr[pl.ds(0, half), :] = scores(pl.ds(half, half), 2 * i + 1)
    v_lo = values(2 * i)
    absorb(s0_scr, 0, 0, half, v_lo, diag_col0=0)
    finish(0, half)
    absorb(s0_scr, half, half, half, v_lo)
    absorb(s1_scr, 0, half, half, values(2 * i + 1), diag_col0=half)
    finish(half, half)


def _attention(q, k, v, batch, seq):
    tq = ATTN_TILE
    nq = seq // tq
    stat = pltpu.VMEM((tq, LANES), _f32)
    score = pltpu.VMEM((tq, tq // 2), _f32)
    return pl.pallas_call(
        functools.partial(_attn_kernel, tile=ATTN_TILE, chunk=ATTN_CHUNK),
        grid=(batch, N_HEADS, nq),
        in_specs=[pl.BlockSpec((tq, QK_PAD), lambda b, h, i: (b * nq + i, h)),
                  pl.BlockSpec((seq, QK_PAD), lambda b, h, i: (b, h)),
                  pl.BlockSpec((seq, V_HEAD), lambda b, h, i: (b, h))],
        out_specs=pl.BlockSpec((tq, V_HEAD), lambda b, h, i: (b * nq + i, h)),
        out_shape=jax.ShapeDtypeStruct((batch * seq, N_HEADS * V_HEAD), _bf16),
        scratch_shapes=[score, score, stat, stat, stat],
        compiler_params=_params(("arbitrary", "arbitrary", "arbitrary")),
        name="attention",
    )(q, k, v)


def _merge_kernel(x_ref, attn_ref, mixed_ref, sga_ref, sgb_ref, w_a_ref, w_b_ref, w_o_ref, o_ref):
    y_a = _dot(mixed_ref[...], w_a_ref[...])
    y_b = _dot(attn_ref[...], w_b_ref[...])
    merged = sga_ref[...].astype(_f32) * y_a + sgb_ref[...].astype(_f32) * y_b
    o_ref[...] = x_ref[...] + _dot(merged.astype(_bf16), w_o_ref[...])


def _merge(x2, attn, mixed, sga, sgb, w_a, w_b, w_o):
    t = x2.shape[0]
    tm = ROW_TILE

    def row(width):
        return pl.BlockSpec((tm, width), lambda i: (i, 0))

    return pl.pallas_call(
        _merge_kernel,
        grid=(t // tm,),
        in_specs=[row(D_MODEL), row(D_MODEL), row(POOL_WIDTH), row(D_MODEL), row(D_MODEL),
                  _resident(w_a.shape), _resident(w_b.shape), _resident(w_o.shape)],
        out_specs=row(D_MODEL),
        out_shape=jax.ShapeDtypeStruct((t, D_MODEL), _f32),
        compiler_params=_params(("arbitrary",)),
        name="merge",
    )(x2, attn, mixed, sga, sgb, w_a, w_b, w_o)


def _ffn_kernel(x_ref, p_ref, gf_ref, w_gate_ref, w_up_ref, w_down_ref, gp_ref, w_pg_ref, w_ple_ref, gl_ref,
                o_ref, *, final):
    x = x_ref[...]
    h = _rms(x, gf_ref[...]).astype(_bf16)
    acc = None
    for c in range(D_FF // FF_CHUNK):
        cols = slice(c * FF_CHUNK, (c + 1) * FF_CHUNK)
        gate = _dot(h, w_gate_ref[:, cols])
        up = _dot(h, w_up_ref[:, cols])
        act = (gate * jax.nn.sigmoid(gate) * up).astype(_bf16)
        part = _dot(act, w_down_ref[cols, :])
        acc = part if acc is None else acc + part
    x = x + acc
    hp = _rms(x, gp_ref[...]).astype(_bf16)
    gate = jax.nn.sigmoid(_dot(hp, w_pg_ref[...]))
    x = x + gate * _dot(p_ref[...].astype(_bf16), w_ple_ref[...])
    if final:
        x = _rms(x, gl_ref[...])
    o_ref[...] = x


def _ffn(x2, p2, gf, w_gate, w_up, w_down, gp, w_pg, w_ple, gl, final):
    t = x2.shape[0]
    tm = ROW_TILE

    def row(width):
        return pl.BlockSpec((tm, width), lambda i: (i, 0))

    return pl.pallas_call(
        functools.partial(_ffn_kernel, final=final),
        grid=(t // tm,),
        in_specs=[row(D_MODEL), row(PLE_DIM), _resident(gf.shape), _resident(w_gate.shape),
                  _resident(w_up.shape), _resident(w_down.shape), _resident(gp.shape),
                  _resident(w_pg.shape), _resident(w_ple.shape), _resident(gl.shape)],
        out_specs=row(D_MODEL),
        out_shape=jax.ShapeDtypeStruct((t, D_MODEL), _f32),
        compiler_params=_params(("arbitrary",)),
        name="ffn_ple",
    )(x2, p2, gf, w_gate, w_up, w_down, gp, w_pg, w_ple, gl)


def _rotated(w):
    half = QK_ROPE // 2
    return jnp.concatenate([-w[..., half:], w[..., :half]], axis=-1)


def _prep_w_in(w_in):
    kr = w_in[:, OFF_KR:OFF_KR + QK_ROPE]
    return jnp.concatenate([w_in[:, :OFF_KR], kr, _rotated(kr), w_in[:, OFF_KR + QK_ROPE:]], axis=1).astype(_bf16)


def _prep_w_uq(w_uq):
    scale = QK_HEAD ** -0.5 * LOG2E
    pe = w_uq[..., QK_NOPE:]
    w = jnp.concatenate([w_uq[..., :QK_NOPE], pe, _rotated(pe)], axis=-1) * scale
    return w.reshape(Q_LORA, N_HEADS * QK_PAD).astype(_bf16)


def _prep_w_ukv(w_ukv):
    k_part = w_ukv[..., :QK_NOPE].reshape(KV_LORA, N_HEADS * QK_NOPE)
    v_part = w_ukv[..., QK_NOPE:].reshape(KV_LORA, N_HEADS * V_HEAD)
    return jnp.concatenate([k_part, v_part], axis=1).astype(_bf16)


def kernel(x, p, positions, norm_mix, w_in, w_pool, pool_scale, q_norm, kv_norm, w_uq, w_ukv, w_a, w_b, w_o,
           norm_ffn, w_gate, w_up, w_down, norm_ple, w_ple_gate, w_ple, final_norm):
    batch, seq, _ = x.shape
    depth = w_in.shape[0]
    t = batch * seq
    assert seq % ROW_TILE == 0 and seq % ATTN_TILE == 0 and ATTN_TILE % (2 * ATTN_CHUNK) == 0

    inv_freq = 1.0 / (ROPE_THETA ** (jnp.arange(0, QK_ROPE, 2, dtype=_f32) / QK_ROPE))
    zeros = jnp.zeros((QK_ROPE,), _f32)
    inv_row = jnp.concatenate([inv_freq, inv_freq, zeros])[None, :]
    msk_row = jnp.concatenate([jnp.ones((QK_ROPE,), _f32), zeros])[None, :]
    pos_col = positions.astype(_f32).reshape(t, 1)
    cos_t, sin_t = _rope_tables(pos_col, inv_row, msk_row, 2048)

    def vec(a):
        return a.reshape(1, -1)

    x2 = x.reshape(t, D_MODEL)
    for i in range(depth):
        mixed, q, k, v, sga, sgb = _front(
            x2, vec(norm_mix[i]), _prep_w_in(w_in[i]), w_pool[i].astype(_bf16), vec(pool_scale[i]),
            vec(q_norm[i]), vec(kv_norm[i]), _prep_w_uq(w_uq[i]), _prep_w_ukv(w_ukv[i]), cos_t, sin_t, seq)
        attn = _attention(q, k, v, batch, seq)
        x2 = _merge(x2, attn, mixed, sga, sgb, w_a[i].astype(_bf16), w_b[i].astype(_bf16), w_o[i].astype(_bf16))
        x2 = _ffn(x2, p[i].reshape(t, PLE_DIM), vec(norm_ffn[i]), w_gate[i].astype(_bf16), w_up[i].astype(_bf16),
                  w_down[i].astype(_bf16), vec(norm_ple[i]), w_ple_gate[i].astype(_bf16), w_ple[i].astype(_bf16),
                  vec(final_norm), final=(i == depth - 1))
    return x2.reshape(batch, seq, D_MODEL)
```

```python
import functools

import jax
import jax.numpy as jnp
from jax import lax
from jax.experimental import pallas as pl
from jax.experimental.pallas import tpu as pltpu

D_MODEL = 1024
PLE_DIM = 256
POOL_WINDOWS = (2, 4, 8, 16)
POOL_GROUP = 128
POOL_WIDTH = 512
N_HEADS = 8
Q_LORA = 512
KV_LORA = 256
QK_NOPE = 128
QK_ROPE = 64
QK_HEAD = QK_NOPE + QK_ROPE
V_HEAD = 128
D_FF = 2816
ROPE_THETA = 10000.0
EPS = 1e-6

LANES = 128
QK_PAD = 2 * LANES
POOL_HALO = 16

OFF_U = 0
OFF_CQ = OFF_U + POOL_WIDTH
OFF_CKV = OFF_CQ + Q_LORA
OFF_KR = OFF_CKV + KV_LORA
OFF_GA = OFF_KR + LANES
OFF_GB = OFF_GA + D_MODEL
IN_WIDTH_PAD = OFF_GB + D_MODEL

ROW_TILE = 512
ATTN_TILE = 1024
ATTN_CHUNK = 256
FF_CHUNK = 256
VMEM_LIMIT = 56 * 1024 * 1024

MASK_VALUE = -1e30
LOG2E = 1.4426950408889634

_f32 = jnp.float32
_bf16 = jnp.bfloat16


def _dot(a, b):
    return jnp.dot(a, b, preferred_element_type=_f32)


def _rms(x, g):
    return x * lax.rsqrt(jnp.mean(x * x, axis=-1, keepdims=True) + EPS) * g


def _resident(shape):
    nd = len(shape)
    return pl.BlockSpec(shape, lambda *_: (0,) * nd, pipeline_mode=pl.Buffered(1))


def _params(sem):
    return pltpu.CompilerParams(dimension_semantics=sem, vmem_limit_bytes=VMEM_LIMIT)


def _rope_table_kernel(pos_ref, inv_ref, msk_ref, cos_ref, sin_ref):
    ang = pos_ref[...] * inv_ref[...]
    cos_ref[...] = jnp.cos(ang) * msk_ref[...]
    sin_ref[...] = jnp.sin(ang) * msk_ref[...]


def _rope_tables(pos_col, inv_row, msk_row, tm):
    t = pos_col.shape[0]
    row = pl.BlockSpec((tm, LANES), lambda i: (i, 0))
    return pl.pallas_call(
        _rope_table_kernel,
        grid=(t // tm,),
        in_specs=[pl.BlockSpec((tm, 1), lambda i: (i, 0)), _resident((1, LANES)), _resident((1, LANES))],
        out_specs=[row, row],
        out_shape=[jax.ShapeDtypeStruct((t, LANES), _f32)] * 2,
        compiler_params=_params(("arbitrary",)),
        name="rope_tables",
    )(pos_col, inv_row, msk_row)


def _front_kernel(x_ref, g_ref, w_in_ref, w_pool_ref, ps_ref, qg_ref, kvg_ref, w_uq_ref, w_ukv_ref,
                  cos_ref, sin_ref,
                  mixed_ref, q_ref, k_ref, v_ref, sga_ref, sgb_ref,
                  carry_ref, *, tm, seq):
    i = pl.program_id(0)
    t0 = (i * tm) % seq
    h = _rms(x_ref[...], g_ref[...]).astype(_bf16)

    u = _dot(h, w_in_ref[:, OFF_U:OFF_U + POOL_WIDTH])

    @pl.when(t0 == 0)
    def _():
        carry_ref[...] = jnp.zeros_like(carry_ref)

    ext = jnp.concatenate([carry_ref[...], u], axis=0)
    carry_ref[...] = u[tm - POOL_HALO:, :]
    sums = []
    cur = ext
    step = 1
    for _ in POOL_WINDOWS:
        cur = cur + pltpu.roll(cur, step, axis=0)
        sums.append(cur[POOL_HALO:, :POOL_GROUP])
        cur = cur[:, POOL_GROUP:]
        step *= 2
    t_idx = t0 + lax.broadcasted_iota(jnp.int32, (tm, POOL_GROUP), 0)
    for g, w in enumerate(POOL_WINDOWS):
        cols = slice(g * POOL_GROUP, (g + 1) * POOL_GROUP)
        cnt = jnp.minimum(t_idx + 1, w).astype(_f32)
        pooled = sums[g] / cnt - u[:, cols]
        mixed = _dot(pooled.astype(_bf16), w_pool_ref[g]) * ps_ref[:, cols]
        mixed_ref[:, cols] = mixed.astype(_bf16)

    cos_t = cos_ref[...]
    sin_t = sin_ref[...]

    def rope(v):
        return v * cos_t + pltpu.roll(v, QK_ROPE, axis=1) * sin_t

    cq = _dot(h, w_in_ref[:, OFF_CQ:OFF_CQ + Q_LORA])
    cqn = _rms(cq, qg_ref[...]).astype(_bf16)
    for hd in range(N_HEADS):
        qh = _dot(cqn, w_uq_ref[:, hd * QK_PAD:(hd + 1) * QK_PAD])
        q_ref[:, hd * QK_PAD:hd * QK_PAD + LANES] = qh[:, :LANES].astype(_bf16)
        q_ref[:, hd * QK_PAD + LANES:(hd + 1) * QK_PAD] = rope(qh[:, LANES:]).astype(_bf16)

    ckv = _dot(h, w_in_ref[:, OFF_CKV:OFF_CKV + KV_LORA])
    ckvn = _rms(ckv, kvg_ref[...]).astype(_bf16)
    k_pe = rope(_dot(h, w_in_ref[:, OFF_KR:OFF_KR + LANES])).astype(_bf16)
    for hd in range(N_HEADS):
        kh = _dot(ckvn, w_ukv_ref[:, hd * LANES:(hd + 1) * LANES])
        k_ref[:, hd * QK_PAD:hd * QK_PAD + LANES] = kh.astype(_bf16)
        k_ref[:, hd * QK_PAD + LANES:(hd + 1) * QK_PAD] = k_pe
    v_off = N_HEADS * QK_NOPE
    v_ref[...] = _dot(ckvn, w_ukv_ref[:, v_off:v_off + N_HEADS * V_HEAD]).astype(_bf16)

    sga_ref[...] = jax.nn.sigmoid(_dot(h, w_in_ref[:, OFF_GA:OFF_GA + D_MODEL])).astype(_bf16)
    sgb_ref[...] = jax.nn.sigmoid(_dot(h, w_in_ref[:, OFF_GB:OFF_GB + D_MODEL])).astype(_bf16)


def _front(x2, g, w_in, w_pool, ps, qg, kvg, w_uq, w_ukv, cos_t, sin_t, seq):
    t = x2.shape[0]
    tm = ROW_TILE

    def row(width):
        return pl.BlockSpec((tm, width), lambda i: (i, 0))

    out_widths = (POOL_WIDTH, N_HEADS * QK_PAD, N_HEADS * QK_PAD, N_HEADS * V_HEAD, D_MODEL, D_MODEL)
    return pl.pallas_call(
        functools.partial(_front_kernel, tm=tm, seq=seq),
        grid=(t // tm,),
        in_specs=[row(D_MODEL), _resident(g.shape), _resident(w_in.shape), _resident(w_pool.shape),
                  _resident(ps.shape), _resident(qg.shape), _resident(kvg.shape), _resident(w_uq.shape),
                  _resident(w_ukv.shape), row(LANES), row(LANES)],
        out_specs=[row(w) for w in out_widths],
        out_shape=[jax.ShapeDtypeStruct((t, w), _bf16) for w in out_widths],
        scratch_shapes=[pltpu.VMEM((POOL_HALO, POOL_WIDTH), _f32)],
        compiler_params=_params(("arbitrary",)),
        name="front",
    )(x2, g, w_in, w_pool, ps, qg, kvg, w_uq, w_ukv, cos_t, sin_t)


def _attn_kernel(q_ref, k_ref, v_ref, o_ref, s0_scr, s1_scr, m_scr, l_scr, acc_scr, *, tile, chunk):
    i = pl.program_id(2)
    half = tile // 2

    def scores(q_row0, nq, blk):
        start = pl.multiple_of(blk * half, half)
        return lax.dot_general(k_ref[pl.ds(start, half), :], q_ref[pl.ds(q_row0, nq), :],
                               (((1,), (1,)), ((), ())), preferred_element_type=_f32)

    def values(blk):
        return v_ref[pl.ds(pl.multiple_of(blk * half, half), half), :]

    def absorb(s_ref, s_col0, col0, ncols, v, key0=None):
        for c in range(ncols // chunk):
            cols = pl.ds(col0 + c * chunk, chunk)
            s = s_ref[:, pl.ds(s_col0 + c * chunk, chunk)]
            if key0 is not None:
                key_id = key0 + lax.broadcasted_iota(jnp.int32, s.shape, 0)
                q_id = col0 + c * chunk + lax.broadcasted_iota(jnp.int32, s.shape, 1)
                s = jnp.where(key_id <= q_id, s, MASK_VALUE)
            m_old = m_scr[:, cols]
            m_new = jnp.maximum(m_old, jnp.max(s, axis=0, keepdims=True))
            alpha = jnp.exp2(m_old - m_new)
            p = jnp.exp2(s - m_new)
            l_scr[:, cols] = alpha * l_scr[:, cols] + jnp.sum(p, axis=0, keepdims=True)
            m_scr[:, cols] = m_new
            pv = lax.dot_general(v, p.astype(_bf16), (((0,), (0,)), ((), ())),
                                 preferred_element_type=_f32)
            acc_scr[:, cols] = alpha * acc_scr[:, cols] + pv

    def finish(col0, ncols):
        cols = pl.ds(col0, ncols)
        o_ref[cols, :] = (acc_scr[:, cols] / l_scr[:, cols]).T.astype(_bf16)

    m_scr[...] = jnp.full(m_scr.shape, MASK_VALUE, _f32)
    l_scr[...] = jnp.zeros(l_scr.shape, _f32)
    acc_scr[...] = jnp.zeros(acc_scr.shape, _f32)
    s0_scr[...] = scores(0, tile, 0)

    def body(t, carry):
        s1_scr[...] = scores(0, tile, 2 * t + 1)
        absorb(s0_scr, 0, 0, tile, values(2 * t))
        s0_scr[...] = scores(0, tile, 2 * t + 2)
        absorb(s1_scr, 0, 0, tile, values(2 * t + 1))
        return carry

    lax.fori_loop(0, i, body, 0)

    s1_scr[:, pl.ds(0, half)] = scores(half, half, 2 * i + 1)
    v_lo = values(2 * i)
    absorb(s0_scr, 0, 0, half, v_lo, key0=0)
    finish(0, half)
    absorb(s0_scr, half, half, half, v_lo)
    absorb(s1_scr, 0, half, half, values(2 * i + 1), key0=half)
    finish(half, half)


def _attention(q, k, v, batch, seq):
    tq = ATTN_TILE
    nq = seq // tq
    stat = pltpu.VMEM((1, tq), _f32)
    score = pltpu.VMEM((tq // 2, tq), _f32)
    return pl.pallas_call(
        functools.partial(_attn_kernel, tile=ATTN_TILE, chunk=ATTN_CHUNK),
        grid=(batch, N_HEADS, nq),
        in_specs=[pl.BlockSpec((tq, QK_PAD), lambda b, h, i: (b * nq + i, h)),
                  pl.BlockSpec((seq, QK_PAD), lambda b, h, i: (b, h)),
                  pl.BlockSpec((seq, V_HEAD), lambda b, h, i: (b, h))],
        out_specs=pl.BlockSpec((tq, V_HEAD), lambda b, h, i: (b * nq + i, h)),
        out_shape=jax.ShapeDtypeStruct((batch * seq, N_HEADS * V_HEAD), _bf16),
        scratch_shapes=[score, score, stat, stat, pltpu.VMEM((V_HEAD, tq), _f32)],
        compiler_params=_params(("arbitrary", "arbitrary", "arbitrary")),
        name="attention",
    )(q, k, v)


def _merge_kernel(x_ref, attn_ref, mixed_ref, sga_ref, sgb_ref, w_a_ref, w_b_ref, w_o_ref, o_ref):
    y_a = _dot(mixed_ref[...], w_a_ref[...])
    y_b = _dot(attn_ref[...], w_b_ref[...])
    merged = sga_ref[...].astype(_f32) * y_a + sgb_ref[...].astype(_f32) * y_b
    o_ref[...] = x_ref[...] + _dot(merged.astype(_bf16), w_o_ref[...])


def _merge(x2, attn, mixed, sga, sgb, w_a, w_b, w_o):
    t = x2.shape[0]
    tm = ROW_TILE

    def row(width):
        return pl.BlockSpec((tm, width), lambda i: (i, 0))

    return pl.pallas_call(
        _merge_kernel,
        grid=(t // tm,),
        in_specs=[row(D_MODEL), row(D_MODEL), row(POOL_WIDTH), row(D_MODEL), row(D_MODEL),
                  _resident(w_a.shape), _resident(w_b.shape), _resident(w_o.shape)],
        out_specs=row(D_MODEL),
        out_shape=jax.ShapeDtypeStruct((t, D_MODEL), _f32),
        compiler_params=_params(("arbitrary",)),
        name="merge",
    )(x2, attn, mixed, sga, sgb, w_a, w_b, w_o)


def _ffn_kernel(x_ref, p_ref, gf_ref, w_gate_ref, w_up_ref, w_down_ref, gp_ref, w_pg_ref, w_ple_ref, gl_ref,
                o_ref, *, final):
    x = x_ref[...]
    h = _rms(x, gf_ref[...]).astype(_bf16)
    acc = None
    for c in range(D_FF // FF_CHUNK):
        cols = slice(c * FF_CHUNK, (c + 1) * FF_CHUNK)
        gate = _dot(h, w_gate_ref[:, cols])
        up = _dot(h, w_up_ref[:, cols])
        act = (gate * jax.nn.sigmoid(gate) * up).astype(_bf16)
        part = _dot(act, w_down_ref[cols, :])
        acc = part if acc is None else acc + part
    x = x + acc
    hp = _rms(x, gp_ref[...]).astype(_bf16)
    gate = jax.nn.sigmoid(_dot(hp, w_pg_ref[...]))
    x = x + gate * _dot(p_ref[...].astype(_bf16), w_ple_ref[...])
    if final:
        x = _rms(x, gl_ref[...])
    o_ref[...] = x


def _ffn(x2, p2, gf, w_gate, w_up, w_down, gp, w_pg, w_ple, gl, final):
    t = x2.shape[0]
    tm = ROW_TILE

    def row(width):
        return pl.BlockSpec((tm, width), lambda i: (i, 0))

    return pl.pallas_call(
        functools.partial(_ffn_kernel, final=final),
        grid=(t // tm,),
        in_specs=[row(D_MODEL), row(PLE_DIM), _resident(gf.shape), _resident(w_gate.shape),
                  _resident(w_up.shape), _resident(w_down.shape), _resident(gp.shape),
                  _resident(w_pg.shape), _resident(w_ple.shape), _resident(gl.shape)],
        out_specs=row(D_MODEL),
        out_shape=jax.ShapeDtypeStruct((t, D_MODEL), _f32),
        compiler_params=_params(("arbitrary",)),
        name="ffn_ple",
    )(x2, p2, gf, w_gate, w_up, w_down, gp, w_pg, w_ple, gl)


def _rotated(w):
    half = QK_ROPE // 2
    return jnp.concatenate([-w[..., half:], w[..., :half]], axis=-1)


def _prep_w_in(w_in):
    kr = w_in[:, OFF_KR:OFF_KR + QK_ROPE]
    return jnp.concatenate([w_in[:, :OFF_KR], kr, _rotated(kr), w_in[:, OFF_KR + QK_ROPE:]], axis=1).astype(_bf16)


def _prep_w_uq(w_uq):
    scale = QK_HEAD ** -0.5 * LOG2E
    pe = w_uq[..., QK_NOPE:]
    w = jnp.concatenate([w_uq[..., :QK_NOPE], pe, _rotated(pe)], axis=-1) * scale
    return w.reshape(Q_LORA, N_HEADS * QK_PAD).astype(_bf16)


def _prep_w_ukv(w_ukv):
    k_part = w_ukv[..., :QK_NOPE].reshape(KV_LORA, N_HEADS * QK_NOPE)
    v_part = w_ukv[..., QK_NOPE:].reshape(KV_LORA, N_HEADS * V_HEAD)
    return jnp.concatenate([k_part, v_part], axis=1).astype(_bf16)


def kernel(x, p, positions, norm_mix, w_in, w_pool, pool_scale, q_norm, kv_norm, w_uq, w_ukv, w_a, w_b, w_o,
           norm_ffn, w_gate, w_up, w_down, norm_ple, w_ple_gate, w_ple, final_norm):
    batch, seq, _ = x.shape
    depth = w_in.shape[0]
    t = batch * seq
    assert seq % ROW_TILE == 0 and seq % ATTN_TILE == 0 and ATTN_TILE % (2 * ATTN_CHUNK) == 0

    inv_freq = 1.0 / (ROPE_THETA ** (jnp.arange(0, QK_ROPE, 2, dtype=_f32) / QK_ROPE))
    zeros = jnp.zeros((QK_ROPE,), _f32)
    inv_row = jnp.concatenate([inv_freq, inv_freq, zeros])[None, :]
    msk_row = jnp.concatenate([jnp.ones((QK_ROPE,), _f32), zeros])[None, :]
    pos_col = positions.astype(_f32).reshape(t, 1)
    cos_t, sin_t = _rope_tables(pos_col, inv_row, msk_row, 2048)

    def vec(a):
        return a.reshape(1, -1)

    x2 = x.reshape(t, D_MODEL)
    for i in range(depth):
        mixed, q, k, v, sga, sgb = _front(
            x2, vec(norm_mix[i]), _prep_w_in(w_in[i]), w_pool[i].astype(_bf16), vec(pool_scale[i]),
            vec(q_norm[i]), vec(kv_norm[i]), _prep_w_uq(w_uq[i]), _prep_w_ukv(w_ukv[i]), cos_t, sin_t, seq)
        attn = _attention(q, k, v, batch, seq)
        x2 = _merge(x2, attn, mixed, sga, sgb, w_a[i].astype(_bf16), w_b[i].astype(_bf16), w_o[i].astype(_bf16))
        x2 = _ffn(x2, p[i].reshape(t, PLE_DIM), vec(norm_ffn[i]), w_gate[i].astype(_bf16), w_up[i].astype(_bf16),
                  w_down[i].astype(_bf16), vec(norm_ple[i]), w_ple_gate[i].astype(_bf16), w_ple[i].astype(_bf16),
                  vec(final_norm), final=(i == depth - 1))
    return x2.reshape(batch, seq, D_MODEL)
```

```python
import functools

import jax
import jax.numpy as jnp
from jax import lax
from jax.experimental import pallas as pl
from jax.experimental.pallas import tpu as pltpu

D_MODEL = 1024
PLE_DIM = 256
POOL_WINDOWS = (2, 4, 8, 16)
POOL_GROUP = 128
POOL_WIDTH = 512
N_HEADS = 8
Q_LORA = 512
KV_LORA = 256
QK_NOPE = 128
QK_ROPE = 64
QK_HEAD = QK_NOPE + QK_ROPE
V_HEAD = 128
D_FF = 2816
ROPE_THETA = 10000.0
EPS = 1e-6

LANES = 128
QK_PAD = 2 * LANES
POOL_HALO = 16

OFF_U = 0
OFF_CQ = OFF_U + POOL_WIDTH
OFF_CKV = OFF_CQ + Q_LORA
OFF_KR = OFF_CKV + KV_LORA
OFF_GA = OFF_KR + LANES
OFF_GB = OFF_GA + D_MODEL
IN_WIDTH_PAD = OFF_GB + D_MODEL

ROW_TILE = 512
ROPE_TILE = 2048
Q_HEAD_GROUP = 4
ATTN_TILE = 1024
ATTN_CHUNK = 256
FF_CHUNK = 256
VMEM_LIMIT = 56 * 1024 * 1024

MASK_VALUE = -1e30
LOG2E = 1.4426950408889634

_f32 = jnp.float32
_bf16 = jnp.bfloat16
_NT = (((1,), (1,)), ((), ()))


def _dot(a, b):
    return jnp.dot(a, b, preferred_element_type=_f32)


def _dot_nt(a, b):
    return lax.dot_general(a, b, _NT, preferred_element_type=_f32)


def _rms(x, g):
    return x * lax.rsqrt(jnp.mean(x * x, axis=-1, keepdims=True) + EPS) * g


def _resident(shape):
    nd = len(shape)
    return pl.BlockSpec(shape, lambda *_: (0,) * nd, pipeline_mode=pl.Buffered(1))


def _layer(arr, layer):
    nd = arr.ndim - 1
    return pl.BlockSpec((None,) + arr.shape[1:], lambda *_: (layer,) + (0,) * nd, pipeline_mode=pl.Buffered(1))


def _params(sem):
    return pltpu.CompilerParams(dimension_semantics=sem, vmem_limit_bytes=VMEM_LIMIT)


def _rope_table_kernel(pos_ref, inv_ref, msk_ref, cos_ref, sin_ref, cost_ref, sint_ref):
    ang = pos_ref[...] * inv_ref[...]
    c = jnp.cos(ang) * msk_ref[...]
    s = jnp.sin(ang) * msk_ref[...]
    cos_ref[...] = c
    sin_ref[...] = s
    cost_ref[...] = c.T[:QK_ROPE, :]
    sint_ref[...] = s.T[:QK_ROPE, :]


def _rope_tables(pos_col, inv_row, msk_row):
    t = pos_col.shape[0]
    tm = ROPE_TILE
    row = pl.BlockSpec((tm, LANES), lambda i: (i, 0))
    col = pl.BlockSpec((QK_ROPE, tm), lambda i: (0, i))
    return pl.pallas_call(
        _rope_table_kernel,
        grid=(t // tm,),
        in_specs=[pl.BlockSpec((tm, 1), lambda i: (i, 0)), _resident((1, LANES)), _resident((1, LANES))],
        out_specs=[row, row, col, col],
        out_shape=[jax.ShapeDtypeStruct((t, LANES), _f32)] * 2 + [jax.ShapeDtypeStruct((QK_ROPE, t), _f32)] * 2,
        compiler_params=_params(("arbitrary",)),
        name="rope_tables",
    )(pos_col, inv_row, msk_row)


def _front_kernel(x_ref, g_ref, w_in_ref, w_pool_ref, ps_ref, qg_ref, kvg_ref, w_uqt_ref, w_uk_ref, w_vt_ref,
                  cos_ref, sin_ref, cost_ref, sint_ref,
                  mixed_ref, qt_ref, k_ref, vt_ref, sga_ref, sgb_ref,
                  carry_ref, *, tm, seq):
    i = pl.program_id(0)
    t0 = (i * tm) % seq
    h = _rms(x_ref[...], g_ref[...]).astype(_bf16)

    u = _dot(h, w_in_ref[:, OFF_U:OFF_U + POOL_WIDTH])

    @pl.when(t0 == 0)
    def _():
        carry_ref[...] = jnp.zeros_like(carry_ref)

    ext = jnp.concatenate([carry_ref[...], u], axis=0)
    carry_ref[...] = u[tm - POOL_HALO:, :]
    sums = []
    cur = ext
    step = 1
    for _ in POOL_WINDOWS:
        cur = cur + pltpu.roll(cur, step, axis=0)
        sums.append(cur[POOL_HALO:, :POOL_GROUP])
        cur = cur[:, POOL_GROUP:]
        step *= 2
    t_idx = t0 + lax.broadcasted_iota(jnp.int32, (tm, POOL_GROUP), 0)
    for g, w in enumerate(POOL_WINDOWS):
        cols = slice(g * POOL_GROUP, (g + 1) * POOL_GROUP)
        cnt = jnp.minimum(t_idx + 1, w).astype(_f32)
        pooled = sums[g] / cnt - u[:, cols]
        mixed = _dot(pooled.astype(_bf16), w_pool_ref[g]) * ps_ref[:, cols]
        mixed_ref[:, cols] = mixed.astype(_bf16)

    cq = _dot(h, w_in_ref[:, OFF_CQ:OFF_CQ + Q_LORA])
    cqn = _rms(cq, qg_ref[...]).astype(_bf16)
    cos_tt = cost_ref[...]
    sin_tt = sint_ref[...]
    rows_per_group = Q_HEAD_GROUP * QK_PAD
    for g0 in range(0, N_HEADS, Q_HEAD_GROUP):
        qt = _dot_nt(w_uqt_ref[g0 * QK_PAD:g0 * QK_PAD + rows_per_group, :], cqn)
        for j in range(Q_HEAD_GROUP):
            r = j * QK_PAD
            o = (g0 + j) * QK_PAD
            qt_ref[o:o + QK_NOPE, :] = qt[r:r + QK_NOPE].astype(_bf16)
            pe = qt[r + QK_NOPE:r + QK_HEAD] * cos_tt + qt[r + QK_HEAD:r + QK_PAD] * sin_tt
            qt_ref[o + QK_NOPE:o + QK_HEAD, :] = pe.astype(_bf16)
            qt_ref[o + QK_HEAD:o + QK_PAD, :] = jnp.zeros((QK_PAD - QK_HEAD, tm), _bf16)

    ckv = _dot(h, w_in_ref[:, OFF_CKV:OFF_CKV + KV_LORA])
    ckvn = _rms(ckv, kvg_ref[...]).astype(_bf16)
    kr = _dot(h, w_in_ref[:, OFF_KR:OFF_KR + LANES])
    k_pe = (kr * cos_ref[...] + pltpu.roll(kr, QK_ROPE, axis=1) * sin_ref[...]).astype(_bf16)
    for hd in range(N_HEADS):
        kh = _dot(ckvn, w_uk_ref[:, hd * QK_NOPE:(hd + 1) * QK_NOPE])
        k_ref[:, hd * QK_PAD:hd * QK_PAD + QK_NOPE] = kh.astype(_bf16)
        k_ref[:, hd * QK_PAD + QK_NOPE:(hd + 1) * QK_PAD] = k_pe
    vt_ref[...] = _dot_nt(w_vt_ref[...], ckvn).astype(_bf16)

    sga_ref[...] = jax.nn.sigmoid(_dot(h, w_in_ref[:, OFF_GA:OFF_GA + D_MODEL])).astype(_bf16)
    sgb_ref[...] = jax.nn.sigmoid(_dot(h, w_in_ref[:, OFF_GB:OFF_GB + D_MODEL])).astype(_bf16)


def _front(x2, layer, g, w_in, w_pool, ps, qg, kvg, w_uqt, w_uk, w_vt, tables, seq):
    t = x2.shape[0]
    tm = ROW_TILE
    cos_t, sin_t, cos_tt, sin_tt = tables

    def row(width):
        return pl.BlockSpec((tm, width), lambda i: (i, 0))

    def col(height):
        return pl.BlockSpec((height, tm), lambda i: (0, i))

    def row_out(width):
        return jax.ShapeDtypeStruct((t, width), _bf16)

    def col_out(height):
        return jax.ShapeDtypeStruct((height, t), _bf16)

    weights = (g, w_in, w_pool, ps, qg, kvg, w_uqt, w_uk, w_vt)
    return pl.pallas_call(
        functools.partial(_front_kernel, tm=tm, seq=seq),
        grid=(t // tm,),
        in_specs=[row(D_MODEL)] + [_layer(w, layer) for w in weights]
                 + [row(LANES), row(LANES), col(QK_ROPE), col(QK_ROPE)],
        out_specs=[row(POOL_WIDTH), col(N_HEADS * QK_PAD), row(N_HEADS * QK_PAD), col(N_HEADS * V_HEAD),
                   row(D_MODEL), row(D_MODEL)],
        out_shape=[row_out(POOL_WIDTH), col_out(N_HEADS * QK_PAD), row_out(N_HEADS * QK_PAD),
                   col_out(N_HEADS * V_HEAD), row_out(D_MODEL), row_out(D_MODEL)],
        scratch_shapes=[pltpu.VMEM((POOL_HALO, POOL_WIDTH), _f32)],
        compiler_params=_params(("arbitrary",)),
        name="front",
    )(x2, *weights, cos_t, sin_t, cos_tt, sin_tt)


def _attn_kernel(qt_ref, k_ref, vt_ref, o_ref, s0_scr, s1_scr, m_scr, l_scr, acc_scr, *, tile, chunk):
    i = pl.program_id(2)
    half = tile // 2

    def scores(q_col0, nq, blk):
        start = pl.multiple_of(blk * half, half)
        return _dot(k_ref[pl.ds(start, half), :], qt_ref[:, pl.ds(q_col0, nq)])

    def values(blk):
        return vt_ref[:, pl.ds(pl.multiple_of(blk * half, half), half)]

    def absorb(s_ref, s_col0, col0, ncols, vt, key0=None):
        for c in range(ncols // chunk):
            cols = pl.ds(col0 + c * chunk, chunk)
            s = s_ref[:, pl.ds(s_col0 + c * chunk, chunk)]
            if key0 is not None:
                key_id = key0 + lax.broadcasted_iota(jnp.int32, s.shape, 0)
                q_id = col0 + c * chunk + lax.broadcasted_iota(jnp.int32, s.shape, 1)
                s = jnp.where(key_id <= q_id, s, MASK_VALUE)
            m_old = m_scr[:, cols]
            m_new = jnp.maximum(m_old, jnp.max(s, axis=0, keepdims=True))
            alpha = jnp.exp2(m_old - m_new)
            p = jnp.exp2(s - m_new)
            l_scr[:, cols] = alpha * l_scr[:, cols] + jnp.sum(p, axis=0, keepdims=True)
            m_scr[:, cols] = m_new
            acc_scr[:, cols] = alpha * acc_scr[:, cols] + _dot(vt, p.astype(_bf16))

    def finish(col0, ncols):
        cols = pl.ds(col0, ncols)
        o_ref[cols, :] = (acc_scr[:, cols] / l_scr[:, cols]).T.astype(_bf16)

    m_scr[...] = jnp.full(m_scr.shape, MASK_VALUE, _f32)
    l_scr[...] = jnp.zeros(l_scr.shape, _f32)
    acc_scr[...] = jnp.zeros(acc_scr.shape, _f32)
    s0_scr[...] = scores(0, tile, 0)

    def body(t, carry):
        s1_scr[...] = scores(0, tile, 2 * t + 1)
        absorb(s0_scr, 0, 0, tile, values(2 * t))
        s0_scr[...] = scores(0, tile, 2 * t + 2)
        absorb(s1_scr, 0, 0, tile, values(2 * t + 1))
        return carry

    lax.fori_loop(0, i, body, 0)

    s1_scr[:, pl.ds(0, half)] = scores(half, half, 2 * i + 1)
    v_lo = values(2 * i)
    absorb(s0_scr, 0, 0, half, v_lo, key0=0)
    finish(0, half)
    absorb(s0_scr, half, half, half, v_lo)
    absorb(s1_scr, 0, half, half, values(2 * i + 1), key0=half)
    finish(half, half)


def _attention(qt, k, vt, batch, seq):
    tq = ATTN_TILE
    nq = seq // tq
    stat = pltpu.VMEM((1, tq), _f32)
    score = pltpu.VMEM((tq // 2, tq), _f32)
    return pl.pallas_call(
        functools.partial(_attn_kernel, tile=ATTN_TILE, chunk=ATTN_CHUNK),
        grid=(batch, N_HEADS, nq),
        in_specs=[pl.BlockSpec((QK_PAD, tq), lambda b, h, i: (h, b * nq + i)),
                  pl.BlockSpec((seq, QK_PAD), lambda b, h, i: (b, h)),
                  pl.BlockSpec((V_HEAD, seq), lambda b, h, i: (h, b))],
        out_specs=pl.BlockSpec((tq, V_HEAD), lambda b, h, i: (b * nq + i, h)),
        out_shape=jax.ShapeDtypeStruct((batch * seq, N_HEADS * V_HEAD), _bf16),
        scratch_shapes=[score, score, stat, stat, pltpu.VMEM((V_HEAD, tq), _f32)],
        compiler_params=_params(("arbitrary", "arbitrary", "arbitrary")),
        name="attention",
    )(qt, k, vt)


def _merge_kernel(x_ref, attn_ref, mixed_ref, sga_ref, sgb_ref, w_a_ref, w_b_ref, w_o_ref, o_ref):
    y_a = _dot(mixed_ref[...], w_a_ref[...])
    y_b = _dot(attn_ref[...], w_b_ref[...])
    merged = sga_ref[...].astype(_f32) * y_a + sgb_ref[...].astype(_f32) * y_b
    o_ref[...] = x_ref[...] + _dot(merged.astype(_bf16), w_o_ref[...])


def _merge(x2, attn, mixed, sga, sgb, layer, w_a, w_b, w_o):
    t = x2.shape[0]
    tm = ROW_TILE

    def row(width):
        return pl.BlockSpec((tm, width), lambda i: (i, 0))

    return pl.pallas_call(
        _merge_kernel,
        grid=(t // tm,),
        in_specs=[row(D_MODEL), row(D_MODEL), row(POOL_WIDTH), row(D_MODEL), row(D_MODEL),
                  _layer(w_a, layer), _layer(w_b, layer), _layer(w_o, layer)],
        out_specs=row(D_MODEL),
        out_shape=jax.ShapeDtypeStruct((t, D_MODEL), _f32),
        compiler_params=_params(("arbitrary",)),
        name="merge",
    )(x2, attn, mixed, sga, sgb, w_a, w_b, w_o)


def _ffn_kernel(x_ref, p_ref, gf_ref, w_gate_ref, w_up_ref, w_down_ref, gp_ref, w_pg_ref, w_ple_ref, gl_ref,
                o_ref, *, final):
    x = x_ref[...]
    h = _rms(x, gf_ref[...]).astype(_bf16)
    acc = None
    for c in range(D_FF // FF_CHUNK):
        cols = slice(c * FF_CHUNK, (c + 1) * FF_CHUNK)
        gate = _dot(h, w_gate_ref[:, cols])
        up = _dot(h, w_up_ref[:, cols])
        act = (gate * jax.nn.sigmoid(gate) * up).astype(_bf16)
        part = _dot(act, w_down_ref[cols, :])
        acc = part if acc is None else acc + part
    x = x + acc
    hp = _rms(x, gp_ref[...]).astype(_bf16)
    gate = jax.nn.sigmoid(_dot(hp, w_pg_ref[...]))
    x = x + gate * _dot(p_ref[...].astype(_bf16), w_ple_ref[...])
    if final:
        x = _rms(x, gl_ref[...])
    o_ref[...] = x


def _ffn(x2, p3, layer, gf, w_gate, w_up, w_down, gp, w_pg, w_ple, gl, final):
    t = x2.shape[0]
    tm = ROW_TILE
    weights = (gf, w_gate, w_up, w_down, gp, w_pg, w_ple)
    return pl.pallas_call(
        functools.partial(_ffn_kernel, final=final),
        grid=(t // tm,),
        in_specs=[pl.BlockSpec((tm, D_MODEL), lambda i: (i, 0)),
                  pl.BlockSpec((None, tm, PLE_DIM), lambda i: (layer, i, 0))]
                 + [_layer(w, layer) for w in weights] + [_resident(gl.shape)],
        out_specs=pl.BlockSpec((tm, D_MODEL), lambda i: (i, 0)),
        out_shape=jax.ShapeDtypeStruct((t, D_MODEL), _f32),
        compiler_params=_params(("arbitrary",)),
        name="ffn_ple",
    )(x2, p3, *weights, gl)


def _rotated(w):
    half = QK_ROPE // 2
    return jnp.concatenate([-w[..., half:], w[..., :half]], axis=-1)


def _prep_w_in(w_in):
    kr = w_in[..., OFF_KR:OFF_KR + QK_ROPE]
    return jnp.concatenate([w_in[..., :OFF_KR], kr, _rotated(kr), w_in[..., OFF_KR + QK_ROPE:]],
                           axis=-1).astype(_bf16)


def _prep_w_uq_t(w_uq):
    scale = QK_HEAD ** -0.5 * LOG2E
    pe = w_uq[..., QK_NOPE:]
    w = jnp.concatenate([w_uq[..., :QK_NOPE], pe, _rotated(pe)], axis=-1) * scale
    w = w.reshape(w.shape[0], Q_LORA, N_HEADS * QK_PAD)
    return jnp.swapaxes(w, 1, 2).astype(_bf16)


def kernel(x, p, positions, norm_mix, w_in, w_pool, pool_scale, q_norm, kv_norm, w_uq, w_ukv, w_a, w_b, w_o,
           norm_ffn, w_gate, w_up, w_down, norm_ple, w_ple_gate, w_ple, final_norm):
    batch, seq, _ = x.shape
    depth = w_in.shape[0]
    t = batch * seq
    assert seq % ROW_TILE == 0 and seq % ATTN_TILE == 0 and ATTN_TILE % (2 * ATTN_CHUNK) == 0
    assert t % ROPE_TILE == 0 and N_HEADS % Q_HEAD_GROUP == 0

    inv_freq = 1.0 / (ROPE_THETA ** (jnp.arange(0, QK_ROPE, 2, dtype=_f32) / QK_ROPE))
    zeros = jnp.zeros((QK_ROPE,), _f32)
    inv_row = jnp.concatenate([inv_freq, inv_freq, zeros])[None, :]
    msk_row = jnp.concatenate([jnp.ones((QK_ROPE,), _f32), zeros])[None, :]
    tables = _rope_tables(positions.astype(_f32).reshape(t, 1), inv_row, msk_row)

    def vec(a):
        return a[:, None, :]

    def cast(a):
        return a.astype(_bf16)

    g_mix, g_q, g_kv, g_ffn, g_ple = vec(norm_mix), vec(q_norm), vec(kv_norm), vec(norm_ffn), vec(norm_ple)
    ps = vec(pool_scale)
    w_in_p = _prep_w_in(w_in)
    w_pool_b = cast(w_pool)
    w_uqt = _prep_w_uq_t(w_uq)
    w_uk = cast(w_ukv[..., :QK_NOPE].reshape(depth, KV_LORA, N_HEADS * QK_NOPE))
    w_vt = cast(jnp.swapaxes(w_ukv[..., QK_NOPE:].reshape(depth, KV_LORA, N_HEADS * V_HEAD), 1, 2))
    w_a_b, w_b_b, w_o_b = cast(w_a), cast(w_b), cast(w_o)
    w_gate_b, w_up_b, w_down_b = cast(w_gate), cast(w_up), cast(w_down)
    w_pg_b, w_ple_b = cast(w_ple_gate), cast(w_ple)
    gl = final_norm.reshape(1, D_MODEL)
    p3 = p.reshape(depth, t, PLE_DIM)

    x2 = x.reshape(t, D_MODEL)
    for i in range(depth):
        mixed, qt, k, vt, sga, sgb = _front(x2, i, g_mix, w_in_p, w_pool_b, ps, g_q, g_kv, w_uqt, w_uk, w_vt,
                                            tables, seq)
        attn = _attention(qt, k, vt, batch, seq)
        x2 = _merge(x2, attn, mixed, sga, sgb, i, w_a_b, w_b_b, w_o_b)
        x2 = _ffn(x2, p3, i, g_ffn, w_gate_b, w_up_b, w_down_b, g_ple, w_pg_b, w_ple_b, gl,
                  final=(i == depth - 1))
    return x2.reshape(batch, seq, D_MODEL)
```

```python
import functools

import jax
import jax.numpy as jnp
from jax import lax
from jax.experimental import pallas as pl
from jax.experimental.pallas import tpu as pltpu

D_MODEL = 1024
PLE_DIM = 256
POOL_WINDOWS = (2, 4, 8, 16)
POOL_GROUP = 128
POOL_WIDTH = 512
N_HEADS = 8
Q_LORA = 512
KV_LORA = 256
QK_NOPE = 128
QK_ROPE = 64
QK_HEAD = QK_NOPE + QK_ROPE
V_HEAD = 128
D_FF = 2816
ROPE_THETA = 10000.0
EPS = 1e-6

LANES = 128
QK_PAD = 2 * LANES
POOL_HALO = 16

OFF_U = 0
OFF_CQ = OFF_U + POOL_WIDTH
OFF_CKV = OFF_CQ + Q_LORA
OFF_KR = OFF_CKV + KV_LORA
OFF_GA = OFF_KR + LANES
OFF_GB = OFF_GA + D_MODEL
IN_WIDTH_PAD = OFF_GB + D_MODEL

ROW_TILE = 512
ROPE_TILE = 2048
Q_HEAD_GROUP = 4
ATTN_TILE = 2048
ATTN_BLOCK = 512
ATTN_CHUNK = 256
FF_CHUNK = 256
VMEM_LIMIT = 56 * 1024 * 1024

MASK_VALUE = -1e30
LOG2E = 1.4426950408889634

_f32 = jnp.float32
_bf16 = jnp.bfloat16
_NT = (((1,), (1,)), ((), ()))


def _dot(a, b):
    return jnp.dot(a, b, preferred_element_type=_f32)


def _dot_nt(a, b):
    return lax.dot_general(a, b, _NT, preferred_element_type=_f32)


def _rms(x, g):
    return x * lax.rsqrt(jnp.mean(x * x, axis=-1, keepdims=True) + EPS) * g


def _resident(shape):
    nd = len(shape)
    return pl.BlockSpec(shape, lambda *_: (0,) * nd, pipeline_mode=pl.Buffered(1))


def _layer(arr, layer):
    nd = arr.ndim - 1
    return pl.BlockSpec((None,) + arr.shape[1:], lambda *_: (layer,) + (0,) * nd, pipeline_mode=pl.Buffered(1))


def _params(sem):
    return pltpu.CompilerParams(dimension_semantics=sem, vmem_limit_bytes=VMEM_LIMIT)


def _rope_table_kernel(pos_ref, inv_ref, msk_ref, cos_ref, sin_ref, cost_ref, sint_ref):
    ang = pos_ref[...] * inv_ref[...]
    c = jnp.cos(ang) * msk_ref[...]
    s = jnp.sin(ang) * msk_ref[...]
    cos_ref[...] = c
    sin_ref[...] = s
    cost_ref[...] = c.T[:QK_ROPE, :]
    sint_ref[...] = s.T[:QK_ROPE, :]


def _rope_tables(pos_col, inv_row, msk_row):
    t = pos_col.shape[0]
    tm = ROPE_TILE
    row = pl.BlockSpec((tm, LANES), lambda i: (i, 0))
    col = pl.BlockSpec((QK_ROPE, tm), lambda i: (0, i))
    return pl.pallas_call(
        _rope_table_kernel,
        grid=(t // tm,),
        in_specs=[pl.BlockSpec((tm, 1), lambda i: (i, 0)), _resident((1, LANES)), _resident((1, LANES))],
        out_specs=[row, row, col, col],
        out_shape=[jax.ShapeDtypeStruct((t, LANES), _f32)] * 2 + [jax.ShapeDtypeStruct((QK_ROPE, t), _f32)] * 2,
        compiler_params=_params(("arbitrary",)),
        name="rope_tables",
    )(pos_col, inv_row, msk_row)


def _front_kernel(x_ref, g_ref, w_in_ref, w_pool_ref, ps_ref, qg_ref, kvg_ref, w_uqt_ref, w_uk_ref, w_vt_ref,
                  cos_ref, sin_ref, cost_ref, sint_ref,
                  mixed_ref, qt_ref, k_ref, vt_ref, sga_ref, sgb_ref,
                  carry_ref, *, tm, seq):
    i = pl.program_id(0)
    t0 = (i * tm) % seq
    h = _rms(x_ref[...], g_ref[...]).astype(_bf16)

    u = _dot(h, w_in_ref[:, OFF_U:OFF_U + POOL_WIDTH])

    @pl.when(t0 == 0)
    def _():
        carry_ref[...] = jnp.zeros_like(carry_ref)

    ext = jnp.concatenate([carry_ref[...], u], axis=0)
    carry_ref[...] = u[tm - POOL_HALO:, :]
    sums = []
    cur = ext
    step = 1
    for _ in POOL_WINDOWS:
        cur = cur + pltpu.roll(cur, step, axis=0)
        sums.append(cur[POOL_HALO:, :POOL_GROUP])
        cur = cur[:, POOL_GROUP:]
        step *= 2
    t_idx = t0 + lax.broadcasted_iota(jnp.int32, (tm, POOL_GROUP), 0)
    for g, w in enumerate(POOL_WINDOWS):
        cols = slice(g * POOL_GROUP, (g + 1) * POOL_GROUP)
        cnt = jnp.minimum(t_idx + 1, w).astype(_f32)
        pooled = sums[g] / cnt - u[:, cols]
        mixed = _dot(pooled.astype(_bf16), w_pool_ref[g]) * ps_ref[:, cols]
        mixed_ref[:, cols] = mixed.astype(_bf16)

    cq = _dot(h, w_in_ref[:, OFF_CQ:OFF_CQ + Q_LORA])
    cqn = _rms(cq, qg_ref[...]).astype(_bf16)
    cos_tt = cost_ref[...]
    sin_tt = sint_ref[...]
    rows_per_group = Q_HEAD_GROUP * QK_PAD
    for g0 in range(0, N_HEADS, Q_HEAD_GROUP):
        qt = _dot_nt(w_uqt_ref[g0 * QK_PAD:g0 * QK_PAD + rows_per_group, :], cqn)
        for j in range(Q_HEAD_GROUP):
            r = j * QK_PAD
            o = (g0 + j) * QK_PAD
            qt_ref[o:o + QK_NOPE, :] = qt[r:r + QK_NOPE].astype(_bf16)
            pe = qt[r + QK_NOPE:r + QK_HEAD] * cos_tt + qt[r + QK_HEAD:r + QK_PAD] * sin_tt
            qt_ref[o + QK_NOPE:o + QK_HEAD, :] = pe.astype(_bf16)
            qt_ref[o + QK_HEAD:o + QK_PAD, :] = jnp.zeros((QK_PAD - QK_HEAD, tm), _bf16)

    ckv = _dot(h, w_in_ref[:, OFF_CKV:OFF_CKV + KV_LORA])
    ckvn = _rms(ckv, kvg_ref[...]).astype(_bf16)
    kr = _dot(h, w_in_ref[:, OFF_KR:OFF_KR + LANES])
    k_pe = (kr * cos_ref[...] + pltpu.roll(kr, QK_ROPE, axis=1) * sin_ref[...]).astype(_bf16)
    k_nope = _dot(ckvn, w_uk_ref[...]).astype(_bf16)
    for hd in range(N_HEADS):
        k_ref[:, hd * QK_PAD:hd * QK_PAD + QK_NOPE] = k_nope[:, hd * QK_NOPE:(hd + 1) * QK_NOPE]
        k_ref[:, hd * QK_PAD + QK_NOPE:(hd + 1) * QK_PAD] = k_pe
    vt_ref[...] = _dot_nt(w_vt_ref[...], ckvn).astype(_bf16)

    sga_ref[...] = jax.nn.sigmoid(_dot(h, w_in_ref[:, OFF_GA:OFF_GA + D_MODEL])).astype(_bf16)
    sgb_ref[...] = jax.nn.sigmoid(_dot(h, w_in_ref[:, OFF_GB:OFF_GB + D_MODEL])).astype(_bf16)


def _front(x2, layer, g, w_in, w_pool, ps, qg, kvg, w_uqt, w_uk, w_vt, tables, seq):
    t = x2.shape[0]
    tm = ROW_TILE
    cos_t, sin_t, cos_tt, sin_tt = tables

    def row(width):
        return pl.BlockSpec((tm, width), lambda i: (i, 0))

    def col(height):
        return pl.BlockSpec((height, tm), lambda i: (0, i))

    def row_out(width):
        return jax.ShapeDtypeStruct((t, width), _bf16)

    def col_out(height):
        return jax.ShapeDtypeStruct((height, t), _bf16)

    weights = (g, w_in, w_pool, ps, qg, kvg, w_uqt, w_uk, w_vt)
    return pl.pallas_call(
        functools.partial(_front_kernel, tm=tm, seq=seq),
        grid=(t // tm,),
        in_specs=[row(D_MODEL)] + [_layer(w, layer) for w in weights]
                 + [row(LANES), row(LANES), col(QK_ROPE), col(QK_ROPE)],
        out_specs=[row(POOL_WIDTH), col(N_HEADS * QK_PAD), row(N_HEADS * QK_PAD), col(N_HEADS * V_HEAD),
                   row(D_MODEL), row(D_MODEL)],
        out_shape=[row_out(POOL_WIDTH), col_out(N_HEADS * QK_PAD), row_out(N_HEADS * QK_PAD),
                   col_out(N_HEADS * V_HEAD), row_out(D_MODEL), row_out(D_MODEL)],
        scratch_shapes=[pltpu.VMEM((POOL_HALO, POOL_WIDTH), _f32)],
        compiler_params=_params(("arbitrary",)),
        name="front",
    )(x2, *weights, cos_t, sin_t, cos_tt, sin_tt)


def _attn_kernel(qt_ref, k_ref, vt_ref, o_ref, s0_scr, s1_scr, m_scr, l_scr, acc_scr, *, tile, block,
                 chunk):
    i = pl.program_id(2)
    nb = tile // block

    def scores(q_col0, nq, blk):
        start = pl.multiple_of(blk * block, block)
        return _dot(k_ref[pl.ds(start, block), :], qt_ref[:, pl.ds(q_col0, nq)])

    def values(blk):
        return vt_ref[:, pl.ds(pl.multiple_of(blk * block, block), block)]

    def absorb(s_ref, s_col0, col0, ncols, vt, key0=None):
        for c in range(ncols // chunk):
            cols = pl.ds(col0 + c * chunk, chunk)
            s = s_ref[:, pl.ds(s_col0 + c * chunk, chunk)]
            if key0 is not None:
                key_id = key0 + lax.broadcasted_iota(jnp.int32, s.shape, 0)
                q_id = col0 + c * chunk + lax.broadcasted_iota(jnp.int32, s.shape, 1)
                s = jnp.where(key_id <= q_id, s, MASK_VALUE)
            m_old = m_scr[:, cols]
            m_new = jnp.maximum(m_old, jnp.max(s, axis=0, keepdims=True))
            alpha = jnp.exp2(m_old - m_new)
            p = jnp.exp2(s - m_new)
            l_scr[:, cols] = alpha * l_scr[:, cols] + jnp.sum(p, axis=0, keepdims=True)
            m_scr[:, cols] = m_new
            acc_scr[:, cols] = alpha * acc_scr[:, cols] + _dot(vt, p.astype(_bf16))

    def finish(col0, ncols):
        cols = pl.ds(col0, ncols)
        o_ref[cols, :] = (acc_scr[:, cols] / l_scr[:, cols]).T.astype(_bf16)

    m_scr[...] = jnp.full(m_scr.shape, MASK_VALUE, _f32)
    l_scr[...] = jnp.zeros(l_scr.shape, _f32)
    acc_scr[...] = jnp.zeros(acc_scr.shape, _f32)
    s0_scr[...] = scores(0, tile, 0)

    def body(t, carry):
        s1_scr[...] = scores(0, tile, 2 * t + 1)
        absorb(s0_scr, 0, 0, tile, values(2 * t))
        s0_scr[...] = scores(0, tile, 2 * t + 2)
        absorb(s1_scr, 0, 0, tile, values(2 * t + 1))
        return carry

    lax.fori_loop(0, i * (nb // 2), body, 0)

    bufs = (s0_scr, s1_scr)
    for kb in range(nb):
        cur, nxt = bufs[kb % 2], bufs[(kb + 1) % 2]
        q0 = kb * block
        if kb + 1 < nb:
            nxt[:, pl.ds(0, tile - q0 - block)] = scores(q0 + block, tile - q0 - block, nb * i + kb + 1)
        vt = values(nb * i + kb)
        absorb(cur, 0, q0, block, vt, key0=q0)
        finish(q0, block)
        if kb + 1 < nb:
            absorb(cur, block, q0 + block, tile - q0 - block, vt)


def _attention(qt, k, vt, batch, seq):
    tq = ATTN_TILE
    nq = seq // tq
    stat = pltpu.VMEM((1, tq), _f32)
    score = pltpu.VMEM((ATTN_BLOCK, tq), _f32)
    return pl.pallas_call(
        functools.partial(_attn_kernel, tile=ATTN_TILE, block=ATTN_BLOCK, chunk=ATTN_CHUNK),
        grid=(batch, N_HEADS, nq),
        in_specs=[pl.BlockSpec((QK_PAD, tq), lambda b, h, i: (h, b * nq + i)),
                  pl.BlockSpec((seq, QK_PAD), lambda b, h, i: (b, h)),
                  pl.BlockSpec((V_HEAD, seq), lambda b, h, i: (h, b))],
        out_specs=pl.BlockSpec((tq, V_HEAD), lambda b, h, i: (b * nq + i, h)),
        out_shape=jax.ShapeDtypeStruct((batch * seq, N_HEADS * V_HEAD), _bf16),
        scratch_shapes=[score, score, stat, stat, pltpu.VMEM((V_HEAD, tq), _f32)],
        compiler_params=_params(("arbitrary", "arbitrary", "arbitrary")),
        name="attention",
    )(qt, k, vt)


def _merge_kernel(x_ref, attn_ref, mixed_ref, sga_ref, sgb_ref, w_a_ref, w_b_ref, w_o_ref, o_ref):
    y_a = _dot(mixed_ref[...], w_a_ref[...])
    y_b = _dot(attn_ref[...], w_b_ref[...])
    merged = sga_ref[...].astype(_f32) * y_a + sgb_ref[...].astype(_f32) * y_b
    o_ref[...] = x_ref[...] + _dot(merged.astype(_bf16), w_o_ref[...])


def _merge(x2, attn, mixed, sga, sgb, layer, w_a, w_b, w_o):
    t = x2.shape[0]
    tm = ROW_TILE

    def row(width):
        return pl.BlockSpec((tm, width), lambda i: (i, 0))

    return pl.pallas_call(
        _merge_kernel,
        grid=(t // tm,),
        in_specs=[row(D_MODEL), row(D_MODEL), row(POOL_WIDTH), row(D_MODEL), row(D_MODEL),
                  _layer(w_a, layer), _layer(w_b, layer), _layer(w_o, layer)],
        out_specs=row(D_MODEL),
        out_shape=jax.ShapeDtypeStruct((t, D_MODEL), _f32),
        compiler_params=_params(("arbitrary",)),
        name="merge",
    )(x2, attn, mixed, sga, sgb, w_a, w_b, w_o)


def _ffn_kernel(x_ref, p_ref, gf_ref, w_gate_ref, w_up_ref, w_down_ref, gp_ref, w_pg_ref, w_ple_ref, gl_ref,
                o_ref, *, final):
    x = x_ref[...]
    h = _rms(x, gf_ref[...]).astype(_bf16)
    acc = None
    for c in range(D_FF // FF_CHUNK):
        cols = slice(c * FF_CHUNK, (c + 1) * FF_CHUNK)
        gate = _dot(h, w_gate_ref[:, cols])
        up = _dot(h, w_up_ref[:, cols])
        act = (gate * jax.nn.sigmoid(gate) * up).astype(_bf16)
        part = _dot(act, w_down_ref[cols, :])
        acc = part if acc is None else acc + part
    x = x + acc
    hp = _rms(x, gp_ref[...]).astype(_bf16)
    gate = jax.nn.sigmoid(_dot(hp, w_pg_ref[...]))
    x = x + gate * _dot(p_ref[...].astype(_bf16), w_ple_ref[...])
    if final:
        x = _rms(x, gl_ref[...])
    o_ref[...] = x


def _ffn(x2, p3, layer, gf, w_gate, w_up, w_down, gp, w_pg, w_ple, gl, final):
    t = x2.shape[0]
    tm = ROW_TILE
    weights = (gf, w_gate, w_up, w_down, gp, w_pg, w_ple)
    return pl.pallas_call(
        functools.partial(_ffn_kernel, final=final),
        grid=(t // tm,),
        in_specs=[pl.BlockSpec((tm, D_MODEL), lambda i: (i, 0)),
                  pl.BlockSpec((None, tm, PLE_DIM), lambda i: (layer, i, 0))]
                 + [_layer(w, layer) for w in weights] + [_resident(gl.shape)],
        out_specs=pl.BlockSpec((tm, D_MODEL), lambda i: (i, 0)),
        out_shape=jax.ShapeDtypeStruct((t, D_MODEL), _f32),
        compiler_params=_params(("arbitrary",)),
        name="ffn_ple",
    )(x2, p3, *weights, gl)


def _rotated(w):
    half = QK_ROPE // 2
    return jnp.concatenate([-w[..., half:], w[..., :half]], axis=-1)


def _prep_w_in(w_in):
    w = w_in.astype(_bf16)
    kr = w[..., OFF_KR:OFF_KR + QK_ROPE]
    return jnp.concatenate([w[..., :OFF_KR], kr, _rotated(kr), w[..., OFF_KR + QK_ROPE:]], axis=-1)


def _prep_w_uq_t(w_uq):
    scale = QK_HEAD ** -0.5 * LOG2E
    w = (w_uq * scale).astype(_bf16)
    pe = w[..., QK_NOPE:]
    w = jnp.concatenate([w[..., :QK_NOPE], pe, _rotated(pe)], axis=-1)
    return jnp.swapaxes(w.reshape(w.shape[0], Q_LORA, N_HEADS * QK_PAD), 1, 2)


def kernel(x, p, positions, norm_mix, w_in, w_pool, pool_scale, q_norm, kv_norm, w_uq, w_ukv, w_a, w_b, w_o,
           norm_ffn, w_gate, w_up, w_down, norm_ple, w_ple_gate, w_ple, final_norm):
    batch, seq, _ = x.shape
    depth = w_in.shape[0]
    t = batch * seq
    assert seq % ROW_TILE == 0 and seq % ATTN_TILE == 0 and ATTN_TILE % (2 * ATTN_BLOCK) == 0 and ATTN_BLOCK % ATTN_CHUNK == 0
    assert t % ROPE_TILE == 0 and N_HEADS % Q_HEAD_GROUP == 0

    inv_freq = 1.0 / (ROPE_THETA ** (jnp.arange(0, QK_ROPE, 2, dtype=_f32) / QK_ROPE))
    zeros = jnp.zeros((QK_ROPE,), _f32)
    inv_row = jnp.concatenate([inv_freq, inv_freq, zeros])[None, :]
    msk_row = jnp.concatenate([jnp.ones((QK_ROPE,), _f32), zeros])[None, :]
    tables = _rope_tables(positions.astype(_f32).reshape(t, 1), inv_row, msk_row)

    def vec(a):
        return a[:, None, :]

    def cast(a):
        return a.astype(_bf16)

    g_mix, g_q, g_kv, g_ffn, g_ple = vec(norm_mix), vec(q_norm), vec(kv_norm), vec(norm_ffn), vec(norm_ple)
    ps = vec(pool_scale)
    w_in_p = _prep_w_in(w_in)
    w_pool_b = cast(w_pool)
    w_uqt = _prep_w_uq_t(w_uq)
    w_ukv_b = cast(w_ukv)
    w_uk = w_ukv_b[..., :QK_NOPE].reshape(depth, KV_LORA, N_HEADS * QK_NOPE)
    w_vt = jnp.swapaxes(w_ukv_b[..., QK_NOPE:].reshape(depth, KV_LORA, N_HEADS * V_HEAD), 1, 2)
    w_a_b, w_b_b, w_o_b = cast(w_a), cast(w_b), cast(w_o)
    w_gate_b, w_up_b, w_down_b = cast(w_gate), cast(w_up), cast(w_down)
    w_pg_b, w_ple_b = cast(w_ple_gate), cast(w_ple)
    gl = final_norm.reshape(1, D_MODEL)
    p3 = p.reshape(depth, t, PLE_DIM)

    x2 = x.reshape(t, D_MODEL)
    for i in range(depth):
        mixed, qt, k, vt, sga, sgb = _front(x2, i, g_mix, w_in_p, w_pool_b, ps, g_q, g_kv, w_uqt, w_uk, w_vt,
                                            tables, seq)
        attn = _attention(qt, k, vt, batch, seq)
        x2 = _merge(x2, attn, mixed, sga, sgb, i, w_a_b, w_b_b, w_o_b)
        x2 = _ffn(x2, p3, i, g_ffn, w_gate_b, w_up_b, w_down_b, g_ple, w_pg_b, w_ple_b, gl,
                  final=(i == depth - 1))
    return x2.reshape(batch, seq, D_MODEL)
```

```python
import functools

import jax
import jax.numpy as jnp
from jax import lax
from jax.experimental import pallas as pl
from jax.experimental.pallas import tpu as pltpu

D_MODEL = 1024
PLE_DIM = 256
POOL_WINDOWS = (2, 4, 8, 16)
POOL_GROUP = 128
POOL_WIDTH = 512
N_HEADS = 8
Q_LORA = 512
KV_LORA = 256
QK_NOPE = 128
QK_ROPE = 64
QK_HEAD = QK_NOPE + QK_ROPE
V_HEAD = 128
D_FF = 2816
ROPE_THETA = 10000.0
EPS = 1e-6

LANES = 128
QK_PAD = 2 * LANES
POOL_HALO = 16

OFF_U = 0
OFF_CQ = OFF_U + POOL_WIDTH
OFF_CKV = OFF_CQ + Q_LORA
OFF_KR = OFF_CKV + KV_LORA
OFF_GA = OFF_KR + LANES
OFF_GB = OFF_GA + D_MODEL
IN_WIDTH_PAD = OFF_GB + D_MODEL

ROW_TILE = 512
ROPE_TILE = 2048
W_IN_LAYOUT_ROWS = 256
Q_HEAD_GROUP = 4
ATTN_TILE = 2048
ATTN_BLOCK = 512
ATTN_CHUNK = 256
SUM_ROWS = 16
FF_CHUNK = 256
VMEM_LIMIT = 56 * 1024 * 1024

MASK_VALUE = -1e30
LOG2E = 1.4426950408889634

_f32 = jnp.float32
_bf16 = jnp.bfloat16
_NT = (((1,), (1,)), ((), ()))


def _dot(a, b):
    return jnp.dot(a, b, preferred_element_type=_f32)


def _dot_nt(a, b):
    return lax.dot_general(a, b, _NT, preferred_element_type=_f32)


def _rms(x, g):
    return x * lax.rsqrt(jnp.mean(x * x, axis=-1, keepdims=True) + EPS) * g


def _resident(shape):
    nd = len(shape)
    return pl.BlockSpec(shape, lambda *_: (0,) * nd, pipeline_mode=pl.Buffered(1))


def _layer(arr, layer):
    nd = arr.ndim - 1
    return pl.BlockSpec((None,) + arr.shape[1:], lambda *_: (layer,) + (0,) * nd, pipeline_mode=pl.Buffered(1))


def _params(sem):
    return pltpu.CompilerParams(dimension_semantics=sem, vmem_limit_bytes=VMEM_LIMIT)


def _rope_table_kernel(pos_ref, inv_ref, msk_ref, cos_ref, sin_ref, cost_ref, sint_ref):
    ang = pos_ref[...] * inv_ref[...]
    c = jnp.cos(ang) * msk_ref[...]
    s = jnp.sin(ang) * msk_ref[...]
    cos_ref[...] = c
    sin_ref[...] = s
    cost_ref[...] = c.T[:QK_ROPE, :]
    sint_ref[...] = s.T[:QK_ROPE, :]


def _rope_tables(pos_col, inv_row, msk_row):
    t = pos_col.shape[0]
    tm = ROPE_TILE
    row = pl.BlockSpec((tm, LANES), lambda i: (i, 0))
    col = pl.BlockSpec((QK_ROPE, tm), lambda i: (0, i))
    return pl.pallas_call(
        _rope_table_kernel,
        grid=(t // tm,),
        in_specs=[pl.BlockSpec((tm, 1), lambda i: (i, 0)), _resident((1, LANES)), _resident((1, LANES))],
        out_specs=[row, row, col, col],
        out_shape=[jax.ShapeDtypeStruct((t, LANES), _f32)] * 2 + [jax.ShapeDtypeStruct((QK_ROPE, t), _f32)] * 2,
        compiler_params=_params(("arbitrary",)),
        name="rope_tables",
    )(pos_col, inv_row, msk_row)


def _front_kernel(x_ref, g_ref, w_in_ref, w_pool_ref, ps_ref, qg_ref, kvg_ref, w_uqt_ref, w_uk_ref, w_vt_ref,
                  cos_ref, sin_ref, cost_ref, sint_ref,
                  mixed_ref, qt_ref, k_ref, vt_ref, sga_ref, sgb_ref,
                  carry_ref, *, tm, seq):
    i = pl.program_id(0)
    t0 = (i * tm) % seq
    h = _rms(x_ref[...], g_ref[...]).astype(_bf16)

    u = _dot(h, w_in_ref[:, OFF_U:OFF_U + POOL_WIDTH])

    @pl.when(t0 == 0)
    def _():
        carry_ref[...] = jnp.zeros_like(carry_ref)

    ext = jnp.concatenate([carry_ref[...], u], axis=0)
    carry_ref[...] = u[tm - POOL_HALO:, :]
    sums = []
    cur = ext
    step = 1
    for _ in POOL_WINDOWS:
        cur = cur + pltpu.roll(cur, step, axis=0)
        sums.append(cur[POOL_HALO:, :POOL_GROUP])
        cur = cur[:, POOL_GROUP:]
        step *= 2
    t_idx = t0 + lax.broadcasted_iota(jnp.int32, (tm, POOL_GROUP), 0)
    for g, w in enumerate(POOL_WINDOWS):
        cols = slice(g * POOL_GROUP, (g + 1) * POOL_GROUP)
        cnt = jnp.minimum(t_idx + 1, w).astype(_f32)
        pooled = sums[g] / cnt - u[:, cols]
        mixed = _dot(pooled.astype(_bf16), w_pool_ref[g]) * ps_ref[:, cols]
        mixed_ref[:, cols] = mixed.astype(_bf16)

    cq = _dot(h, w_in_ref[:, OFF_CQ:OFF_CQ + Q_LORA])
    cqn = _rms(cq, qg_ref[...]).astype(_bf16)
    cos_tt = cost_ref[...]
    sin_tt = sint_ref[...]
    rows_per_group = Q_HEAD_GROUP * QK_PAD
    for g0 in range(0, N_HEADS, Q_HEAD_GROUP):
        qt = _dot_nt(w_uqt_ref[g0 * QK_PAD:g0 * QK_PAD + rows_per_group, :], cqn)
        for j in range(Q_HEAD_GROUP):
            r = j * QK_PAD
            o = (g0 + j) * QK_PAD
            qt_ref[o:o + QK_NOPE, :] = qt[r:r + QK_NOPE].astype(_bf16)
            pe = qt[r + QK_NOPE:r + QK_HEAD] * cos_tt + qt[r + QK_HEAD:r + QK_PAD] * sin_tt
            qt_ref[o + QK_NOPE:o + QK_HEAD, :] = pe.astype(_bf16)
            qt_ref[o + QK_HEAD:o + QK_PAD, :] = jnp.zeros((QK_PAD - QK_HEAD, tm), _bf16)

    ckv = _dot(h, w_in_ref[:, OFF_CKV:OFF_CKV + KV_LORA])
    ckvn = _rms(ckv, kvg_ref[...]).astype(_bf16)
    kr = _dot(h, w_in_ref[:, OFF_KR:OFF_KR + LANES])
    k_pe = (kr * cos_ref[...] + pltpu.roll(kr, QK_ROPE, axis=1) * sin_ref[...]).astype(_bf16)
    k_nope = _dot(ckvn, w_uk_ref[...]).astype(_bf16)
    for hd in range(N_HEADS):
        k_ref[:, hd * QK_PAD:hd * QK_PAD + QK_NOPE] = k_nope[:, hd * QK_NOPE:(hd + 1) * QK_NOPE]
        k_ref[:, hd * QK_PAD + QK_NOPE:(hd + 1) * QK_PAD] = k_pe
    vt_ref[...] = _dot_nt(w_vt_ref[...], ckvn).astype(_bf16)

    sga_ref[...] = jax.nn.sigmoid(_dot(h, w_in_ref[:, OFF_GA:OFF_GA + D_MODEL])).astype(_bf16)
    sgb_ref[...] = jax.nn.sigmoid(_dot(h, w_in_ref[:, OFF_GB:OFF_GB + D_MODEL])).astype(_bf16)


def _front(x2, layer, g, w_in, w_pool, ps, qg, kvg, w_uqt, w_uk, w_vt, tables, seq):
    t = x2.shape[0]
    tm = ROW_TILE
    cos_t, sin_t, cos_tt, sin_tt = tables

    def row(width):
        return pl.BlockSpec((tm, width), lambda i: (i, 0))

    def col(height):
        return pl.BlockSpec((height, tm), lambda i: (0, i))

    def row_out(width):
        return jax.ShapeDtypeStruct((t, width), _bf16)

    def col_out(height):
        return jax.ShapeDtypeStruct((height, t), _bf16)

    weights = (g, w_in, w_pool, ps, qg, kvg, w_uqt, w_uk, w_vt)
    return pl.pallas_call(
        functools.partial(_front_kernel, tm=tm, seq=seq),
        grid=(t // tm,),
        in_specs=[row(D_MODEL)] + [_layer(w, layer) for w in weights]
                 + [row(LANES), row(LANES), col(QK_ROPE), col(QK_ROPE)],
        out_specs=[row(POOL_WIDTH), col(N_HEADS * QK_PAD), row(N_HEADS * QK_PAD), col(N_HEADS * V_HEAD),
                   row(D_MODEL), row(D_MODEL)],
        out_shape=[row_out(POOL_WIDTH), col_out(N_HEADS * QK_PAD), row_out(N_HEADS * QK_PAD),
                   col_out(N_HEADS * V_HEAD), row_out(D_MODEL), row_out(D_MODEL)],
        scratch_shapes=[pltpu.VMEM((POOL_HALO, POOL_WIDTH), _f32)],
        compiler_params=_params(("arbitrary",)),
        name="front",
    )(x2, *weights, cos_t, sin_t, cos_tt, sin_tt)


def _attn_kernel(qt_ref, k_ref, vt_ref, o_ref, s0_scr, s1_scr, m_scr, acc_scr, *, tile, block,
                 chunk):
    i = pl.program_id(2)
    nb = tile // block

    def scores(q_col0, nq, blk):
        start = pl.multiple_of(blk * block, block)
        return _dot(k_ref[pl.ds(start, block), :], qt_ref[:, pl.ds(q_col0, nq)])

    def values(blk):
        vt = vt_ref[:, pl.ds(pl.multiple_of(blk * block, block), block)]
        return jnp.concatenate([vt, jnp.ones((SUM_ROWS, block), _bf16)], axis=0)

    def absorb(s_ref, s_col0, col0, ncols, vt, key0=None):
        for c in range(ncols // chunk):
            cols = pl.ds(col0 + c * chunk, chunk)
            s = s_ref[:, pl.ds(s_col0 + c * chunk, chunk)]
            if key0 is not None:
                key_id = key0 + lax.broadcasted_iota(jnp.int32, s.shape, 0)
                q_id = col0 + c * chunk + lax.broadcasted_iota(jnp.int32, s.shape, 1)
                s = jnp.where(key_id <= q_id, s, MASK_VALUE)
            m_old = m_scr[:, cols]
            m_new = jnp.maximum(m_old, jnp.max(s, axis=0, keepdims=True))
            alpha = jnp.exp2(m_old - m_new)
            p = jnp.exp2(s - m_new)
            m_scr[:, cols] = m_new
            acc_scr[:, cols] = alpha * acc_scr[:, cols] + _dot(vt, p.astype(_bf16))

    def finish(col0, ncols):
        a = acc_scr[:, pl.ds(col0, ncols)]
        o_ref[pl.ds(col0, ncols), :] = (a[:V_HEAD] / a[V_HEAD:V_HEAD + 1]).T.astype(_bf16)

    m_scr[...] = jnp.full(m_scr.shape, MASK_VALUE, _f32)
    acc_scr[...] = jnp.zeros(acc_scr.shape, _f32)
    s0_scr[...] = scores(0, tile, 0)

    bufs = (s0_scr, s1_scr)

    def body(t, carry):
        for kb in range(nb):
            blk = nb * t + kb
            bufs[(kb + 1) % 2][...] = scores(0, tile, blk + 1)
            absorb(bufs[kb % 2], 0, 0, tile, values(blk))
        return carry

    lax.fori_loop(0, i, body, 0)

    for kb in range(nb):
        cur, nxt = bufs[kb % 2], bufs[(kb + 1) % 2]
        q0 = kb * block
        if kb + 1 < nb:
            nxt[:, pl.ds(0, tile - q0 - block)] = scores(q0 + block, tile - q0 - block, nb * i + kb + 1)
        vt = values(nb * i + kb)
        absorb(cur, 0, q0, block, vt, key0=q0)
        finish(q0, block)
        if kb + 1 < nb:
            absorb(cur, block, q0 + block, tile - q0 - block, vt)


def _attention(qt, k, vt, batch, seq):
    tq = ATTN_TILE
    nq = seq // tq
    stat = pltpu.VMEM((1, tq), _f32)
    score = pltpu.VMEM((ATTN_BLOCK, tq), _f32)
    return pl.pallas_call(
        functools.partial(_attn_kernel, tile=ATTN_TILE, block=ATTN_BLOCK, chunk=ATTN_CHUNK),
        grid=(batch, N_HEADS, nq),
        in_specs=[pl.BlockSpec((QK_PAD, tq), lambda b, h, i: (h, b * nq + i)),
                  pl.BlockSpec((seq, QK_PAD), lambda b, h, i: (b, h)),
                  pl.BlockSpec((V_HEAD, seq), lambda b, h, i: (h, b))],
        out_specs=pl.BlockSpec((tq, V_HEAD), lambda b, h, i: (b * nq + i, h)),
        out_shape=jax.ShapeDtypeStruct((batch * seq, N_HEADS * V_HEAD), _bf16),
        scratch_shapes=[score, score, stat, pltpu.VMEM((V_HEAD + SUM_ROWS, tq), _f32)],
        compiler_params=_params(("arbitrary", "arbitrary", "arbitrary")),
        name="attention",
    )(qt, k, vt)


def _merge_kernel(x_ref, attn_ref, mixed_ref, sga_ref, sgb_ref, w_a_ref, w_b_ref, w_o_ref, o_ref):
    y_a = _dot(mixed_ref[...], w_a_ref[...])
    y_b = _dot(attn_ref[...], w_b_ref[...])
    merged = sga_ref[...].astype(_f32) * y_a + sgb_ref[...].astype(_f32) * y_b
    o_ref[...] = x_ref[...] + _dot(merged.astype(_bf16), w_o_ref[...])


def _merge(x2, attn, mixed, sga, sgb, layer, w_a, w_b, w_o):
    t = x2.shape[0]
    tm = ROW_TILE

    def row(width):
        return pl.BlockSpec((tm, width), lambda i: (i, 0))

    return pl.pallas_call(
        _merge_kernel,
        grid=(t // tm,),
        in_specs=[row(D_MODEL), row(D_MODEL), row(POOL_WIDTH), row(D_MODEL), row(D_MODEL),
                  _layer(w_a, layer), _layer(w_b, layer), _layer(w_o, layer)],
        out_specs=row(D_MODEL),
        out_shape=jax.ShapeDtypeStruct((t, D_MODEL), _f32),
        compiler_params=_params(("arbitrary",)),
        name="merge",
    )(x2, attn, mixed, sga, sgb, w_a, w_b, w_o)


def _ffn_kernel(x_ref, p_ref, gf_ref, w_gate_ref, w_up_ref, w_down_ref, gp_ref, w_pg_ref, w_ple_ref, gl_ref,
                o_ref, *, final):
    x = x_ref[...]
    h = _rms(x, gf_ref[...]).astype(_bf16)
    acc = None
    for c in range(D_FF // FF_CHUNK):
        cols = slice(c * FF_CHUNK, (c + 1) * FF_CHUNK)
        gate = _dot(h, w_gate_ref[:, cols])
        up = _dot(h, w_up_ref[:, cols])
        act = (gate * jax.nn.sigmoid(gate) * up).astype(_bf16)
        part = _dot(act, w_down_ref[cols, :])
        acc = part if acc is None else acc + part
    x = x + acc
    hp = _rms(x, gp_ref[...]).astype(_bf16)
    gate = jax.nn.sigmoid(_dot(hp, w_pg_ref[...]))
    x = x + gate * _dot(p_ref[...].astype(_bf16), w_ple_ref[...])
    if final:
        x = _rms(x, gl_ref[...])
    o_ref[...] = x


def _ffn(x2, p3, layer, gf, w_gate, w_up, w_down, gp, w_pg, w_ple, gl, final):
    t = x2.shape[0]
    tm = ROW_TILE
    weights = (gf, w_gate, w_up, w_down, gp, w_pg, w_ple)
    return pl.pallas_call(
        functools.partial(_ffn_kernel, final=final),
        grid=(t // tm,),
        in_specs=[pl.BlockSpec((tm, D_MODEL), lambda i: (i, 0)),
                  pl.BlockSpec((None, tm, PLE_DIM), lambda i: (layer, i, 0))]
                 + [_layer(w, layer) for w in weights] + [_resident(gl.shape)],
        out_specs=pl.BlockSpec((tm, D_MODEL), lambda i: (i, 0)),
        out_shape=jax.ShapeDtypeStruct((t, D_MODEL), _f32),
        compiler_params=_params(("arbitrary",)),
        name="ffn_ple",
    )(x2, p3, *weights, gl)


def _rotated(w):
    half = QK_ROPE // 2
    return jnp.concatenate([-w[..., half:], w[..., :half]], axis=-1)


def _w_in_layout_kernel(w_ref, o_ref):
    w = w_ref[...]
    kr = w[:, OFF_KR:OFF_KR + QK_ROPE]
    o_ref[...] = jnp.concatenate([w[:, :OFF_KR], kr, _rotated(kr), w[:, OFF_KR + QK_ROPE:]],
                                 axis=1).astype(_bf16)


def _prep_w_in(w_in):
    depth, rows, width = w_in.shape
    tr = W_IN_LAYOUT_ROWS
    return pl.pallas_call(
        _w_in_layout_kernel,
        grid=(depth, rows // tr),
        in_specs=[pl.BlockSpec((None, tr, width), lambda l, r: (l, r, 0))],
        out_specs=pl.BlockSpec((None, tr, IN_WIDTH_PAD), lambda l, r: (l, r, 0)),
        out_shape=jax.ShapeDtypeStruct((depth, rows, IN_WIDTH_PAD), _bf16),
        compiler_params=_params(("arbitrary", "arbitrary")),
        name="w_in_layout",
    )(w_in)


def _prep_w_uq_t(w_uq):
    scale = QK_HEAD ** -0.5 * LOG2E
    w = (w_uq * scale).astype(_bf16)
    pe = w[..., QK_NOPE:]
    w = jnp.concatenate([w[..., :QK_NOPE], pe, _rotated(pe)], axis=-1)
    return jnp.swapaxes(w.reshape(w.shape[0], Q_LORA, N_HEADS * QK_PAD), 1, 2)


def kernel(x, p, positions, norm_mix, w_in, w_pool, pool_scale, q_norm, kv_norm, w_uq, w_ukv, w_a, w_b, w_o,
           norm_ffn, w_gate, w_up, w_down, norm_ple, w_ple_gate, w_ple, final_norm):
    batch, seq, _ = x.shape
    depth = w_in.shape[0]
    t = batch * seq
    assert seq % ROW_TILE == 0 and seq % ATTN_TILE == 0 and ATTN_TILE % (2 * ATTN_BLOCK) == 0 and ATTN_BLOCK % ATTN_CHUNK == 0
    assert t % ROPE_TILE == 0 and N_HEADS % Q_HEAD_GROUP == 0

    inv_freq = 1.0 / (ROPE_THETA ** (jnp.arange(0, QK_ROPE, 2, dtype=_f32) / QK_ROPE))
    zeros = jnp.zeros((QK_ROPE,), _f32)
    inv_row = jnp.concatenate([inv_freq, inv_freq, zeros])[None, :]
    msk_row = jnp.concatenate([jnp.ones((QK_ROPE,), _f32), zeros])[None, :]
    tables = _rope_tables(positions.astype(_f32).reshape(t, 1), inv_row, msk_row)

    def vec(a):
        return a[:, None, :]

    def cast(a):
        return a.astype(_bf16)

    g_mix, g_q, g_kv, g_ffn, g_ple = vec(norm_mix), vec(q_norm), vec(kv_norm), vec(norm_ffn), vec(norm_ple)
    ps = vec(pool_scale)
    w_in_p = _prep_w_in(w_in)
    w_pool_b = cast(w_pool)
    w_uqt = _prep_w_uq_t(w_uq)
    w_ukv_b = cast(w_ukv)
    w_uk = w_ukv_b[..., :QK_NOPE].reshape(depth, KV_LORA, N_HEADS * QK_NOPE)
    w_vt = jnp.swapaxes(w_ukv_b[..., QK_NOPE:].reshape(depth, KV_LORA, N_HEADS * V_HEAD), 1, 2)
    w_a_b, w_b_b, w_o_b = cast(w_a), cast(w_b), cast(w_o)
    w_gate_b, w_up_b, w_down_b = cast(w_gate), cast(w_up), cast(w_down)
    w_pg_b, w_ple_b = cast(w_ple_gate), cast(w_ple)
    gl = final_norm.reshape(1, D_MODEL)
    p3 = p.reshape(depth, t, PLE_DIM)

    x2 = x.reshape(t, D_MODEL)
    for i in range(depth):
        mixed, qt, k, vt, sga, sgb = _front(x2, i, g_mix, w_in_p, w_pool_b, ps, g_q, g_kv, w_uqt, w_uk, w_vt,
                                            tables, seq)
        attn = _attention(qt, k, vt, batch, seq)
        x2 = _merge(x2, attn, mixed, sga, sgb, i, w_a_b, w_b_b, w_o_b)
        x2 = _ffn(x2, p3, i, g_ffn, w_gate_b, w_up_b, w_down_b, g_ple, w_pg_b, w_ple_b, gl,
                  final=(i == depth - 1))
    return x2.reshape(batch, seq, D_MODEL)
```

```python
import functools

import jax
import jax.numpy as jnp
from jax import lax
from jax.experimental import pallas as pl
from jax.experimental.pallas import tpu as pltpu

D_MODEL = 1024
PLE_DIM = 256
POOL_WINDOWS = (2, 4, 8, 16)
POOL_GROUP = 128
POOL_WIDTH = 512
N_HEADS = 8
Q_LORA = 512
KV_LORA = 256
QK_NOPE = 128
QK_ROPE = 64
QK_HEAD = QK_NOPE + QK_ROPE
V_HEAD = 128
D_FF = 2816
ROPE_THETA = 10000.0
EPS = 1e-6

LANES = 128
QK_PAD = 2 * LANES
POOL_HALO = 16

OFF_U = 0
OFF_CQ = OFF_U + POOL_WIDTH
OFF_CKV = OFF_CQ + Q_LORA
OFF_KR = OFF_CKV + KV_LORA
OFF_GA = OFF_KR + LANES
OFF_GB = OFF_GA + D_MODEL
IN_WIDTH_PAD = OFF_GB + D_MODEL

ROW_TILE = 512
ROPE_TILE = 2048
W_IN_LAYOUT_ROWS = 256
Q_HEAD_GROUP = 4
ATTN_TILE = 2048
ATTN_BLOCK = 512
ATTN_CHUNK = 256
SUM_ROWS = 16
FF_CHUNK = 256
VMEM_LIMIT = 56 * 1024 * 1024

MASK_VALUE = -1e30
LOG2E = 1.4426950408889634

_f32 = jnp.float32
_bf16 = jnp.bfloat16
_NT = (((1,), (1,)), ((), ()))


def _dot(a, b):
    return jnp.dot(a, b, preferred_element_type=_f32)


def _dot_nt(a, b):
    return lax.dot_general(a, b, _NT, preferred_element_type=_f32)


def _rms(x, g):
    return x * lax.rsqrt(jnp.mean(x * x, axis=-1, keepdims=True) + EPS) * g


def _resident(shape):
    nd = len(shape)
    return pl.BlockSpec(shape, lambda *_: (0,) * nd, pipeline_mode=pl.Buffered(1))


def _layer(arr, layer):
    nd = arr.ndim - 1
    return pl.BlockSpec((None,) + arr.shape[1:], lambda *_: (layer,) + (0,) * nd, pipeline_mode=pl.Buffered(1))


def _params(sem):
    return pltpu.CompilerParams(dimension_semantics=sem, vmem_limit_bytes=VMEM_LIMIT)


def _rope_table_kernel(pos_ref, inv_ref, cos_ref, sin_ref, cost_ref, sint_ref):
    ang = inv_ref[...] * pos_ref[...]
    c = jnp.cos(ang)
    s = jnp.sin(ang)
    zeros = jnp.zeros((LANES - QK_ROPE, ang.shape[1]), _f32)
    ct = jnp.concatenate([c, c], axis=0)
    st = jnp.concatenate([s, s], axis=0)
    cost_ref[...] = ct
    sint_ref[...] = st
    cos_ref[...] = jnp.concatenate([ct, zeros], axis=0).T
    sin_ref[...] = jnp.concatenate([st, zeros], axis=0).T


def _rope_tables(pos_row, inv_col):
    t = pos_row.shape[1]
    tm = ROPE_TILE
    row = pl.BlockSpec((tm, LANES), lambda i: (i, 0))
    col = pl.BlockSpec((QK_ROPE, tm), lambda i: (0, i))
    return pl.pallas_call(
        _rope_table_kernel,
        grid=(t // tm,),
        in_specs=[pl.BlockSpec((1, tm), lambda i: (0, i)), _resident(inv_col.shape)],
        out_specs=[row, row, col, col],
        out_shape=[jax.ShapeDtypeStruct((t, LANES), _f32)] * 2 + [jax.ShapeDtypeStruct((QK_ROPE, t), _f32)] * 2,
        compiler_params=_params(("arbitrary",)),
        name="rope_tables",
    )(pos_row, inv_col)


def _front_kernel(x_ref, g_ref, w_in_ref, w_pool_ref, ps_ref, qg_ref, kvg_ref, w_uqt_ref, w_uk_ref, w_vt_ref,
                  cos_ref, sin_ref, cost_ref, sint_ref,
                  mixed_ref, qt_ref, k_ref, vt_ref, sga_ref, sgb_ref,
                  carry_ref, *, tm, seq):
    i = pl.program_id(0)
    t0 = (i * tm) % seq
    h = _rms(x_ref[...], g_ref[...]).astype(_bf16)

    u = _dot(h, w_in_ref[:, OFF_U:OFF_U + POOL_WIDTH])

    @pl.when(t0 == 0)
    def _():
        carry_ref[...] = jnp.zeros_like(carry_ref)

    ext = jnp.concatenate([carry_ref[...], u], axis=0)
    carry_ref[...] = u[tm - POOL_HALO:, :]
    sums = []
    cur = ext
    step = 1
    for _ in POOL_WINDOWS:
        cur = cur + pltpu.roll(cur, step, axis=0)
        sums.append(cur[POOL_HALO:, :POOL_GROUP])
        cur = cur[:, POOL_GROUP:]
        step *= 2
    t_idx = t0 + lax.broadcasted_iota(jnp.int32, (tm, POOL_GROUP), 0)
    for g, w in enumerate(POOL_WINDOWS):
        cols = slice(g * POOL_GROUP, (g + 1) * POOL_GROUP)
        cnt = jnp.minimum(t_idx + 1, w).astype(_f32)
        pooled = sums[g] / cnt - u[:, cols]
        mixed = _dot(pooled.astype(_bf16), w_pool_ref[g]) * ps_ref[:, cols]
        mixed_ref[:, cols] = mixed.astype(_bf16)

    cq = _dot(h, w_in_ref[:, OFF_CQ:OFF_CQ + Q_LORA])
    cqn = _rms(cq, qg_ref[...]).astype(_bf16)
    cos_tt = cost_ref[...]
    sin_tt = sint_ref[...]
    rows_per_group = Q_HEAD_GROUP * QK_PAD
    for g0 in range(0, N_HEADS, Q_HEAD_GROUP):
        qt = _dot_nt(w_uqt_ref[g0 * QK_PAD:g0 * QK_PAD + rows_per_group, :], cqn)
        for j in range(Q_HEAD_GROUP):
            r = j * QK_PAD
            o = (g0 + j) * QK_PAD
            qt_ref[o:o + QK_NOPE, :] = qt[r:r + QK_NOPE].astype(_bf16)
            pe = qt[r + QK_NOPE:r + QK_HEAD] * cos_tt + qt[r + QK_HEAD:r + QK_PAD] * sin_tt
            qt_ref[o + QK_NOPE:o + QK_HEAD, :] = pe.astype(_bf16)
            qt_ref[o + QK_HEAD:o + QK_PAD, :] = jnp.zeros((QK_PAD - QK_HEAD, tm), _bf16)

    ckv = _dot(h, w_in_ref[:, OFF_CKV:OFF_CKV + KV_LORA])
    ckvn = _rms(ckv, kvg_ref[...]).astype(_bf16)
    kr = _dot(h, w_in_ref[:, OFF_KR:OFF_KR + LANES])
    k_pe = (kr * cos_ref[...] + pltpu.roll(kr, QK_ROPE, axis=1) * sin_ref[...]).astype(_bf16)
    k_nope = _dot(ckvn, w_uk_ref[...]).astype(_bf16)
    for hd in range(N_HEADS):
        k_ref[:, hd * QK_PAD:hd * QK_PAD + QK_NOPE] = k_nope[:, hd * QK_NOPE:(hd + 1) * QK_NOPE]
        k_ref[:, hd * QK_PAD + QK_NOPE:(hd + 1) * QK_PAD] = k_pe
    vt_ref[...] = _dot_nt(w_vt_ref[...], ckvn).astype(_bf16)

    sga_ref[...] = jax.nn.sigmoid(_dot(h, w_in_ref[:, OFF_GA:OFF_GA + D_MODEL])).astype(_bf16)
    sgb_ref[...] = jax.nn.sigmoid(_dot(h, w_in_ref[:, OFF_GB:OFF_GB + D_MODEL])).astype(_bf16)


def _front(x2, layer, g, w_in, w_pool, ps, qg, kvg, w_uqt, w_uk, w_vt, tables, seq):
    t = x2.shape[0]
    tm = ROW_TILE
    cos_t, sin_t, cos_tt, sin_tt = tables

    def row(width):
        return pl.BlockSpec((tm, width), lambda i: (i, 0))

    def col(height):
        return pl.BlockSpec((height, tm), lambda i: (0, i))

    def row_out(width):
        return jax.ShapeDtypeStruct((t, width), _bf16)

    def col_out(height):
        return jax.ShapeDtypeStruct((height, t), _bf16)

    weights = (g, w_in, w_pool, ps, qg, kvg, w_uqt, w_uk, w_vt)
    return pl.pallas_call(
        functools.partial(_front_kernel, tm=tm, seq=seq),
        grid=(t // tm,),
        in_specs=[row(D_MODEL)] + [_layer(w, layer) for w in weights]
                 + [row(LANES), row(LANES), col(QK_ROPE), col(QK_ROPE)],
        out_specs=[row(POOL_WIDTH), col(N_HEADS * QK_PAD), row(N_HEADS * QK_PAD), col(N_HEADS * V_HEAD),
                   row(D_MODEL), row(D_MODEL)],
        out_shape=[row_out(POOL_WIDTH), col_out(N_HEADS * QK_PAD), row_out(N_HEADS * QK_PAD),
                   col_out(N_HEADS * V_HEAD), row_out(D_MODEL), row_out(D_MODEL)],
        scratch_shapes=[pltpu.VMEM((POOL_HALO, POOL_WIDTH), _f32)],
        compiler_params=_params(("arbitrary",)),
        name="front",
    )(x2, *weights, cos_t, sin_t, cos_tt, sin_tt)


def _attn_kernel(qt_ref, k_ref, vt_ref, o_ref, s0_scr, s1_scr, m_scr, acc_scr, *, tile, block,
                 chunk):
    i = pl.program_id(2)
    nb = tile // block

    def scores(q_col0, nq, blk):
        start = pl.multiple_of(blk * block, block)
        return _dot(k_ref[pl.ds(start, block), :], qt_ref[:, pl.ds(q_col0, nq)])

    def values(blk):
        vt = vt_ref[:, pl.ds(pl.multiple_of(blk * block, block), block)]
        return jnp.concatenate([vt, jnp.ones((SUM_ROWS, block), _bf16)], axis=0)

    def absorb(s_ref, s_col0, col0, ncols, vt, key0=None):
        for c in range(ncols // chunk):
            cols = pl.ds(col0 + c * chunk, chunk)
            if key0 is None:
                s = s_ref[:, pl.ds(s_col0 + c * chunk, chunk)]
                vc = vt
            else:
                nkeys = col0 + (c + 1) * chunk - key0
                s = s_ref[pl.ds(0, nkeys), pl.ds(s_col0 + c * chunk, chunk)]
                vc = vt[:, :nkeys]
                key_id = key0 + lax.broadcasted_iota(jnp.int32, s.shape, 0)
                q_id = col0 + c * chunk + lax.broadcasted_iota(jnp.int32, s.shape, 1)
                s = jnp.where(key_id <= q_id, s, MASK_VALUE)
            m_old = m_scr[:, cols]
            m_new = jnp.maximum(m_old, jnp.max(s, axis=0, keepdims=True))
            alpha = jnp.exp2(m_old - m_new)
            p = jnp.exp2(s - m_new)
            m_scr[:, cols] = m_new
            acc_scr[:, cols] = alpha * acc_scr[:, cols] + _dot(vc, p.astype(_bf16))

    def finish(col0, ncols):
        a = acc_scr[:, pl.ds(col0, ncols)]
        o_ref[pl.ds(col0, ncols), :] = (a[:V_HEAD] / a[V_HEAD:V_HEAD + 1]).T.astype(_bf16)

    m_scr[...] = jnp.full(m_scr.shape, MASK_VALUE, _f32)
    acc_scr[...] = jnp.zeros(acc_scr.shape, _f32)
    s0_scr[...] = scores(0, tile, 0)

    bufs = (s0_scr, s1_scr)

    def body(t, carry):
        for kb in range(nb):
            blk = nb * t + kb
            bufs[(kb + 1) % 2][...] = scores(0, tile, blk + 1)
            absorb(bufs[kb % 2], 0, 0, tile, values(blk))
        return carry

    lax.fori_loop(0, i, body, 0)

    for kb in range(nb):
        cur, nxt = bufs[kb % 2], bufs[(kb + 1) % 2]
        q0 = kb * block
        if kb + 1 < nb:
            nxt[:, pl.ds(0, tile - q0 - block)] = scores(q0 + block, tile - q0 - block, nb * i + kb + 1)
        vt = values(nb * i + kb)
        absorb(cur, 0, q0, block, vt, key0=q0)
        finish(q0, block)
        if kb + 1 < nb:
            absorb(cur, block, q0 + block, tile - q0 - block, vt)


def _attention(qt, k, vt, batch, seq):
    tq = ATTN_TILE
    nq = seq // tq
    stat = pltpu.VMEM((1, tq), _f32)
    score = pltpu.VMEM((ATTN_BLOCK, tq), _f32)
    return pl.pallas_call(
        functools.partial(_attn_kernel, tile=ATTN_TILE, block=ATTN_BLOCK, chunk=ATTN_CHUNK),
        grid=(batch, N_HEADS, nq),
        in_specs=[pl.BlockSpec((QK_PAD, tq), lambda b, h, i: (h, b * nq + i)),
                  pl.BlockSpec((seq, QK_PAD), lambda b, h, i: (b, h)),
                  pl.BlockSpec((V_HEAD, seq), lambda b, h, i: (h, b))],
        out_specs=pl.BlockSpec((tq, V_HEAD), lambda b, h, i: (b * nq + i, h)),
        out_shape=jax.ShapeDtypeStruct((batch * seq, N_HEADS * V_HEAD), _bf16),
        scratch_shapes=[score, score, stat, pltpu.VMEM((V_HEAD + SUM_ROWS, tq), _f32)],
        compiler_params=_params(("arbitrary", "arbitrary", "arbitrary")),
        name="attention",
    )(qt, k, vt)


def _back_kernel(x_ref, attn_ref, mixed_ref, sga_ref, sgb_ref, p_ref, w_a_ref, w_b_ref, w_o_ref,
                 gf_ref, w_gate_ref, w_up_ref, w_down_ref, gp_ref, w_pg_ref, w_ple_ref, gl_ref,
                 o_ref, *, final):
    y_a = _dot(mixed_ref[...], w_a_ref[...])
    y_b = _dot(attn_ref[...], w_b_ref[...])
    merged = sga_ref[...].astype(_f32) * y_a + sgb_ref[...].astype(_f32) * y_b
    x = x_ref[...] + _dot(merged.astype(_bf16), w_o_ref[...])
    h = _rms(x, gf_ref[...]).astype(_bf16)
    acc = None
    for c in range(D_FF // FF_CHUNK):
        cols = slice(c * FF_CHUNK, (c + 1) * FF_CHUNK)
        gate = _dot(h, w_gate_ref[:, cols])
        up = _dot(h, w_up_ref[:, cols])
        act = (gate * jax.nn.sigmoid(gate) * up).astype(_bf16)
        part = _dot(act, w_down_ref[cols, :])
        acc = part if acc is None else acc + part
    x = x + acc
    hp = _rms(x, gp_ref[...]).astype(_bf16)
    gate = jax.nn.sigmoid(_dot(hp, w_pg_ref[...]))
    x = x + gate * _dot(p_ref[...].astype(_bf16), w_ple_ref[...])
    if final:
        x = _rms(x, gl_ref[...])
    o_ref[...] = x


def _back(x2, attn, mixed, sga, sgb, p3, layer, weights, gl, final):
    t = x2.shape[0]
    tm = ROW_TILE

    def row(width):
        return pl.BlockSpec((tm, width), lambda i: (i, 0))

    return pl.pallas_call(
        functools.partial(_back_kernel, final=final),
        grid=(t // tm,),
        in_specs=[row(D_MODEL), row(D_MODEL), row(POOL_WIDTH), row(D_MODEL), row(D_MODEL),
                  pl.BlockSpec((None, tm, PLE_DIM), lambda i: (layer, i, 0))]
                 + [_layer(w, layer) for w in weights] + [_resident(gl.shape)],
        out_specs=row(D_MODEL),
        out_shape=jax.ShapeDtypeStruct((t, D_MODEL), _f32),
        compiler_params=_params(("arbitrary",)),
        name="back",
    )(x2, attn, mixed, sga, sgb, p3, *weights, gl)


def _rotated(w):
    half = QK_ROPE // 2
    return jnp.concatenate([-w[..., half:], w[..., :half]], axis=-1)


def _w_in_layout_kernel(wt_ref, o_ref):
    head = wt_ref[:OFF_KR, :].T
    b = wt_ref[OFF_KR:OFF_KR + LANES, :].T
    tail = wt_ref[OFF_KR + QK_ROPE:, :].T
    lane = lax.broadcasted_iota(jnp.int32, b.shape, 1)
    half = QK_ROPE // 2
    kr = jnp.where(lane < QK_ROPE, b,
                   jnp.where(lane < QK_ROPE + half, -pltpu.roll(b, half, axis=1),
                             pltpu.roll(b, QK_ROPE + half, axis=1)))
    o_ref[...] = jnp.concatenate([head, kr, tail], axis=1).astype(_bf16)


def _prep_w_in(w_in):
    depth, rows, width = w_in.shape
    tr = W_IN_LAYOUT_ROWS
    return pl.pallas_call(
        _w_in_layout_kernel,
        grid=(depth, rows // tr),
        in_specs=[pl.BlockSpec((None, width, tr), lambda l, r: (l, 0, r))],
        out_specs=pl.BlockSpec((None, tr, IN_WIDTH_PAD), lambda l, r: (l, r, 0)),
        out_shape=jax.ShapeDtypeStruct((depth, rows, IN_WIDTH_PAD), _bf16),
        compiler_params=_params(("arbitrary", "arbitrary")),
        name="w_in_layout",
    )(jnp.swapaxes(w_in, 1, 2))


def _prep_w_uq_t(w_uq):
    scale = QK_HEAD ** -0.5 * LOG2E
    w = (w_uq * scale).astype(_bf16)
    pe = w[..., QK_NOPE:]
    w = jnp.concatenate([w[..., :QK_NOPE], pe, _rotated(pe)], axis=-1)
    return jnp.swapaxes(w.reshape(w.shape[0], Q_LORA, N_HEADS * QK_PAD), 1, 2)


def kernel(x, p, positions, norm_mix, w_in, w_pool, pool_scale, q_norm, kv_norm, w_uq, w_ukv, w_a, w_b, w_o,
           norm_ffn, w_gate, w_up, w_down, norm_ple, w_ple_gate, w_ple, final_norm):
    batch, seq, _ = x.shape
    depth = w_in.shape[0]
    t = batch * seq
    assert seq % ROW_TILE == 0 and seq % ATTN_TILE == 0 and ATTN_TILE % (2 * ATTN_BLOCK) == 0 and ATTN_BLOCK % ATTN_CHUNK == 0
    assert t % ROPE_TILE == 0 and N_HEADS % Q_HEAD_GROUP == 0

    inv_freq = 1.0 / (ROPE_THETA ** (jnp.arange(0, QK_ROPE, 2, dtype=_f32) / QK_ROPE))
    tables = _rope_tables(positions.astype(_f32).reshape(1, t), inv_freq[:, None])

    def vec(a):
        return a[:, None, :]

    def cast(a):
        return a.astype(_bf16)

    g_mix, g_q, g_kv, g_ffn, g_ple = vec(norm_mix), vec(q_norm), vec(kv_norm), vec(norm_ffn), vec(norm_ple)
    ps = vec(pool_scale)
    w_in_p = _prep_w_in(w_in)
    w_pool_b = cast(w_pool)
    w_uqt = _prep_w_uq_t(w_uq)
    w_ukv_b = cast(w_ukv)
    w_uk = w_ukv_b[..., :QK_NOPE].reshape(depth, KV_LORA, N_HEADS * QK_NOPE)
    w_vt = jnp.swapaxes(w_ukv_b[..., QK_NOPE:].reshape(depth, KV_LORA, N_HEADS * V_HEAD), 1, 2)
    w_a_b, w_b_b, w_o_b = cast(w_a), cast(w_b), cast(w_o)
    w_gate_b, w_up_b, w_down_b = cast(w_gate), cast(w_up), cast(w_down)
    w_pg_b, w_ple_b = cast(w_ple_gate), cast(w_ple)
    gl = final_norm.reshape(1, D_MODEL)
    p3 = p.reshape(depth, t, PLE_DIM)

    x2 = x.reshape(t, D_MODEL)
    for i in range(depth):
        mixed, qt, k, vt, sga, sgb = _front(x2, i, g_mix, w_in_p, w_pool_b, ps, g_q, g_kv, w_uqt, w_uk, w_vt,
                                            tables, seq)
        attn = _attention(qt, k, vt, batch, seq)
        back_weights = (w_a_b, w_b_b, w_o_b, g_ffn, w_gate_b, w_up_b, w_down_b, g_ple, w_pg_b, w_ple_b)
        x2 = _back(x2, attn, mixed, sga, sgb, p3, i, back_weights, gl, final=(i == depth - 1))
    return x2.reshape(batch, seq, D_MODEL)
```

```python
import functools

import jax
import jax.numpy as jnp
from jax import lax
from jax.experimental import pallas as pl
from jax.experimental.pallas import tpu as pltpu

D_MODEL = 1024
PLE_DIM = 256
POOL_WINDOWS = (2, 4, 8, 16)
POOL_GROUP = 128
POOL_WIDTH = 512
N_HEADS = 8
Q_LORA = 512
KV_LORA = 256
QK_NOPE = 128
QK_ROPE = 64
QK_HEAD = QK_NOPE + QK_ROPE
V_HEAD = 128
D_FF = 2816
ROPE_THETA = 10000.0
EPS = 1e-6

LANES = 128
QK_PAD = 2 * LANES
POOL_HALO = 16

OFF_U = 0
OFF_CQ = OFF_U + POOL_WIDTH
OFF_CKV = OFF_CQ + Q_LORA
OFF_KR = OFF_CKV + KV_LORA
OFF_GA = OFF_KR + LANES
OFF_GB = OFF_GA + D_MODEL
IN_WIDTH_PAD = OFF_GB + D_MODEL

ROW_TILE = 512
ROPE_TILE = 2048
W_IN_LAYOUT_ROWS = 256
Q_HEAD_GROUP = 4
ATTN_TILE = 4096
ATTN_BLOCK = 512
ATTN_BLOCKS_PER_TRIP = 4
ATTN_CHUNK = 256
SUM_ROWS = 16
FF_CHUNK = 256
VMEM_LIMIT = 60 * 1024 * 1024

MASK_VALUE = -1e30
LOG2E = 1.4426950408889634

_f32 = jnp.float32
_bf16 = jnp.bfloat16
_NT = (((1,), (1,)), ((), ()))


def _dot(a, b):
    return jnp.dot(a, b, preferred_element_type=_f32)


def _dot_nt(a, b):
    return lax.dot_general(a, b, _NT, preferred_element_type=_f32)


def _rms(x, g):
    return x * lax.rsqrt(jnp.mean(x * x, axis=-1, keepdims=True) + EPS) * g


def _resident(shape):
    nd = len(shape)
    return pl.BlockSpec(shape, lambda *_: (0,) * nd, pipeline_mode=pl.Buffered(1))


def _layer(arr, layer):
    nd = arr.ndim - 1
    return pl.BlockSpec((None,) + arr.shape[1:], lambda *_: (layer,) + (0,) * nd, pipeline_mode=pl.Buffered(1))


def _params(sem):
    return pltpu.CompilerParams(dimension_semantics=sem, vmem_limit_bytes=VMEM_LIMIT)


def _rope_table_kernel(pos_ref, inv_ref, cos_ref, sin_ref, cost_ref, sint_ref):
    ang = inv_ref[...] * pos_ref[...]
    c = jnp.cos(ang)
    s = jnp.sin(ang)
    zeros = jnp.zeros((LANES - QK_ROPE, ang.shape[1]), _f32)
    ct = jnp.concatenate([c, c], axis=0)
    st = jnp.concatenate([s, s], axis=0)
    cost_ref[...] = ct
    sint_ref[...] = st
    cos_ref[...] = jnp.concatenate([ct, zeros], axis=0).T
    sin_ref[...] = jnp.concatenate([st, zeros], axis=0).T


def _rope_tables(pos_row, inv_col):
    t = pos_row.shape[1]
    tm = ROPE_TILE
    row = pl.BlockSpec((tm, LANES), lambda i: (i, 0))
    col = pl.BlockSpec((QK_ROPE, tm), lambda i: (0, i))
    return pl.pallas_call(
        _rope_table_kernel,
        grid=(t // tm,),
        in_specs=[pl.BlockSpec((1, tm), lambda i: (0, i)), _resident(inv_col.shape)],
        out_specs=[row, row, col, col],
        out_shape=[jax.ShapeDtypeStruct((t, LANES), _f32)] * 2 + [jax.ShapeDtypeStruct((QK_ROPE, t), _f32)] * 2,
        compiler_params=_params(("arbitrary",)),
        name="rope_tables",
    )(pos_row, inv_col)


def _front_kernel(x_ref, g_ref, w_in_ref, w_pool_ref, ps_ref, qg_ref, kvg_ref, w_uqt_ref, w_uk_ref, w_vt_ref,
                  cos_ref, sin_ref, cost_ref, sint_ref,
                  mixed_ref, qt_ref, k_ref, vt_ref, sga_ref, sgb_ref,
                  carry_ref, *, tm, seq):
    i = pl.program_id(0)
    t0 = (i * tm) % seq
    h = _rms(x_ref[...], g_ref[...]).astype(_bf16)

    u = _dot(h, w_in_ref[:, OFF_U:OFF_U + POOL_WIDTH])

    @pl.when(t0 == 0)
    def _():
        carry_ref[...] = jnp.zeros_like(carry_ref)

    ext = jnp.concatenate([carry_ref[...], u], axis=0)
    carry_ref[...] = u[tm - POOL_HALO:, :]
    sums = []
    cur = ext
    step = 1
    for _ in POOL_WINDOWS:
        cur = cur + pltpu.roll(cur, step, axis=0)
        sums.append(cur[POOL_HALO:, :POOL_GROUP])
        cur = cur[:, POOL_GROUP:]
        step *= 2
    t_idx = t0 + lax.broadcasted_iota(jnp.int32, (tm, POOL_GROUP), 0)
    for g, w in enumerate(POOL_WINDOWS):
        cols = slice(g * POOL_GROUP, (g + 1) * POOL_GROUP)
        cnt = jnp.minimum(t_idx + 1, w).astype(_f32)
        pooled = sums[g] / cnt - u[:, cols]
        mixed = _dot(pooled.astype(_bf16), w_pool_ref[g]) * ps_ref[:, cols]
        mixed_ref[:, cols] = mixed.astype(_bf16)

    cq = _dot(h, w_in_ref[:, OFF_CQ:OFF_CQ + Q_LORA])
    cqn = _rms(cq, qg_ref[...]).astype(_bf16)
    cos_tt = cost_ref[...]
    sin_tt = sint_ref[...]
    rows_per_group = Q_HEAD_GROUP * QK_PAD
    for g0 in range(0, N_HEADS, Q_HEAD_GROUP):
        qt = _dot_nt(w_uqt_ref[g0 * QK_PAD:g0 * QK_PAD + rows_per_group, :], cqn)
        for j in range(Q_HEAD_GROUP):
            r = j * QK_PAD
            o = (g0 + j) * QK_PAD
            qt_ref[o:o + QK_NOPE, :] = qt[r:r + QK_NOPE].astype(_bf16)
            pe = qt[r + QK_NOPE:r + QK_HEAD] * cos_tt + qt[r + QK_HEAD:r + QK_PAD] * sin_tt
            qt_ref[o + QK_NOPE:o + QK_HEAD, :] = pe.astype(_bf16)
            qt_ref[o + QK_HEAD:o + QK_PAD, :] = jnp.zeros((QK_PAD - QK_HEAD, tm), _bf16)

    ckv = _dot(h, w_in_ref[:, OFF_CKV:OFF_CKV + KV_LORA])
    ckvn = _rms(ckv, kvg_ref[...]).astype(_bf16)
    kr = _dot(h, w_in_ref[:, OFF_KR:OFF_KR + LANES])
    k_pe = (kr * cos_ref[...] + pltpu.roll(kr, QK_ROPE, axis=1) * sin_ref[...]).astype(_bf16)
    k_nope = _dot(ckvn, w_uk_ref[...]).astype(_bf16)
    for hd in range(N_HEADS):
        k_ref[:, hd * QK_PAD:hd * QK_PAD + QK_NOPE] = k_nope[:, hd * QK_NOPE:(hd + 1) * QK_NOPE]
        k_ref[:, hd * QK_PAD + QK_NOPE:(hd + 1) * QK_PAD] = k_pe
    vt_ref[...] = _dot_nt(w_vt_ref[...], ckvn).astype(_bf16)

    sga_ref[...] = jax.nn.sigmoid(_dot(h, w_in_ref[:, OFF_GA:OFF_GA + D_MODEL])).astype(_bf16)
    sgb_ref[...] = jax.nn.sigmoid(_dot(h, w_in_ref[:, OFF_GB:OFF_GB + D_MODEL])).astype(_bf16)


def _front(x2, layer, g, w_in, w_pool, ps, qg, kvg, w_uqt, w_uk, w_vt, tables, seq):
    t = x2.shape[0]
    tm = ROW_TILE
    cos_t, sin_t, cos_tt, sin_tt = tables

    def row(width):
        return pl.BlockSpec((tm, width), lambda i: (i, 0))

    def col(height):
        return pl.BlockSpec((height, tm), lambda i: (0, i))

    def row_out(width):
        return jax.ShapeDtypeStruct((t, width), _bf16)

    def col_out(height):
        return jax.ShapeDtypeStruct((height, t), _bf16)

    weights = (g, w_in, w_pool, ps, qg, kvg, w_uqt, w_uk, w_vt)
    return pl.pallas_call(
        functools.partial(_front_kernel, tm=tm, seq=seq),
        grid=(t // tm,),
        in_specs=[row(D_MODEL)] + [_layer(w, layer) for w in weights]
                 + [row(LANES), row(LANES), col(QK_ROPE), col(QK_ROPE)],
        out_specs=[row(POOL_WIDTH), col(N_HEADS * QK_PAD), row(N_HEADS * QK_PAD), col(N_HEADS * V_HEAD),
                   row(D_MODEL), row(D_MODEL)],
        out_shape=[row_out(POOL_WIDTH), col_out(N_HEADS * QK_PAD), row_out(N_HEADS * QK_PAD),
                   col_out(N_HEADS * V_HEAD), row_out(D_MODEL), row_out(D_MODEL)],
        scratch_shapes=[pltpu.VMEM((POOL_HALO, POOL_WIDTH), _f32)],
        compiler_params=_params(("arbitrary",)),
        name="front",
    )(x2, *weights, cos_t, sin_t, cos_tt, sin_tt)


def _attn_kernel(qt_ref, k_ref, vt_ref, o_ref, s0_scr, s1_scr, m_scr, acc_scr, *, tile, block,
                 chunk, per_trip):
    i = pl.program_id(2)
    nb = tile // block

    def scores(q_col0, nq, blk):
        start = pl.multiple_of(blk * block, block)
        return _dot(k_ref[pl.ds(start, block), :], qt_ref[:, pl.ds(q_col0, nq)])

    def values(blk):
        vt = vt_ref[:, pl.ds(pl.multiple_of(blk * block, block), block)]
        return jnp.concatenate([vt, jnp.ones((SUM_ROWS, block), _bf16)], axis=0)

    def absorb(s_ref, s_col0, col0, ncols, vt, key0=None):
        for c in range(ncols // chunk):
            cols = pl.ds(col0 + c * chunk, chunk)
            if key0 is None:
                s = s_ref[:, pl.ds(s_col0 + c * chunk, chunk)]
                vc = vt
            else:
                nkeys = col0 + (c + 1) * chunk - key0
                s = s_ref[pl.ds(0, nkeys), pl.ds(s_col0 + c * chunk, chunk)]
                vc = vt[:, :nkeys]
                key_id = key0 + lax.broadcasted_iota(jnp.int32, s.shape, 0)
                q_id = col0 + c * chunk + lax.broadcasted_iota(jnp.int32, s.shape, 1)
                s = jnp.where(key_id <= q_id, s, MASK_VALUE)
            m_old = m_scr[:, cols]
            m_new = jnp.maximum(m_old, jnp.max(s, axis=0, keepdims=True))
            alpha = jnp.exp2(m_old - m_new)
            p = jnp.exp2(s - m_new)
            m_scr[:, cols] = m_new
            acc_scr[:, cols] = alpha * acc_scr[:, cols] + _dot(vc, p.astype(_bf16))

    def finish(col0, ncols):
        a = acc_scr[:, pl.ds(col0, ncols)]
        o_ref[pl.ds(col0, ncols), :] = (a[:V_HEAD] / a[V_HEAD:V_HEAD + 1]).T.astype(_bf16)

    m_scr[...] = jnp.full(m_scr.shape, MASK_VALUE, _f32)
    acc_scr[...] = jnp.zeros(acc_scr.shape, _f32)
    s0_scr[...] = scores(0, tile, 0)

    bufs = (s0_scr, s1_scr)

    def body(t, carry):
        for kb in range(per_trip):
            blk = per_trip * t + kb
            bufs[(kb + 1) % 2][...] = scores(0, tile, blk + 1)
            absorb(bufs[kb % 2], 0, 0, tile, values(blk))
        return carry

    lax.fori_loop(0, i * (nb // per_trip), body, 0)

    for kb in range(nb):
        cur, nxt = bufs[kb % 2], bufs[(kb + 1) % 2]
        q0 = kb * block
        if kb + 1 < nb:
            nxt[:, pl.ds(0, tile - q0 - block)] = scores(q0 + block, tile - q0 - block, nb * i + kb + 1)
        vt = values(nb * i + kb)
        absorb(cur, 0, q0, block, vt, key0=q0)
        finish(q0, block)
        if kb + 1 < nb:
            absorb(cur, block, q0 + block, tile - q0 - block, vt)


def _attention(qt, k, vt, batch, seq):
    tq = ATTN_TILE
    nq = seq // tq
    stat = pltpu.VMEM((1, tq), _f32)
    score = pltpu.VMEM((ATTN_BLOCK, tq), _f32)
    return pl.pallas_call(
        functools.partial(_attn_kernel, tile=ATTN_TILE, block=ATTN_BLOCK, chunk=ATTN_CHUNK,
                          per_trip=ATTN_BLOCKS_PER_TRIP),
        grid=(batch, N_HEADS, nq),
        in_specs=[pl.BlockSpec((QK_PAD, tq), lambda b, h, i: (h, b * nq + i)),
                  pl.BlockSpec((seq, QK_PAD), lambda b, h, i: (b, h)),
                  pl.BlockSpec((V_HEAD, seq), lambda b, h, i: (h, b))],
        out_specs=pl.BlockSpec((tq, V_HEAD), lambda b, h, i: (b * nq + i, h)),
        out_shape=jax.ShapeDtypeStruct((batch * seq, N_HEADS * V_HEAD), _bf16),
        scratch_shapes=[score, score, stat, pltpu.VMEM((V_HEAD + SUM_ROWS, tq), _f32)],
        compiler_params=_params(("arbitrary", "arbitrary", "arbitrary")),
        name="attention",
    )(qt, k, vt)


def _back_kernel(x_ref, attn_ref, mixed_ref, sga_ref, sgb_ref, p_ref, w_a_ref, w_b_ref, w_o_ref,
                 gf_ref, w_gate_ref, w_up_ref, w_down_ref, gp_ref, w_pg_ref, w_ple_ref, gl_ref,
                 o_ref, *, final):
    y_a = _dot(mixed_ref[...], w_a_ref[...])
    y_b = _dot(attn_ref[...], w_b_ref[...])
    merged = sga_ref[...].astype(_f32) * y_a + sgb_ref[...].astype(_f32) * y_b
    x = x_ref[...] + _dot(merged.astype(_bf16), w_o_ref[...])
    h = _rms(x, gf_ref[...]).astype(_bf16)
    acc = None
    for c in range(D_FF // FF_CHUNK):
        cols = slice(c * FF_CHUNK, (c + 1) * FF_CHUNK)
        gate = _dot(h, w_gate_ref[:, cols])
        up = _dot(h, w_up_ref[:, cols])
        act = (gate * jax.nn.sigmoid(gate) * up).astype(_bf16)
        part = _dot(act, w_down_ref[cols, :])
        acc = part if acc is None else acc + part
    x = x + acc
    hp = _rms(x, gp_ref[...]).astype(_bf16)
    gate = jax.nn.sigmoid(_dot(hp, w_pg_ref[...]))
    x = x + gate * _dot(p_ref[...].astype(_bf16), w_ple_ref[...])
    if final:
        x = _rms(x, gl_ref[...])
    o_ref[...] = x


def _back(x2, attn, mixed, sga, sgb, p3, layer, weights, gl, final):
    t = x2.shape[0]
    tm = ROW_TILE

    def row(width):
        return pl.BlockSpec((tm, width), lambda i: (i, 0))

    return pl.pallas_call(
        functools.partial(_back_kernel, final=final),
        grid=(t // tm,),
        in_specs=[row(D_MODEL), row(D_MODEL), row(POOL_WIDTH), row(D_MODEL), row(D_MODEL),
                  pl.BlockSpec((None, tm, PLE_DIM), lambda i: (layer, i, 0))]
                 + [_layer(w, layer) for w in weights] + [_resident(gl.shape)],
        out_specs=row(D_MODEL),
        out_shape=jax.ShapeDtypeStruct((t, D_MODEL), _f32),
        compiler_params=_params(("arbitrary",)),
        name="back",
    )(x2, attn, mixed, sga, sgb, p3, *weights, gl)


def _rotated(w):
    half = QK_ROPE // 2
    return jnp.concatenate([-w[..., half:], w[..., :half]], axis=-1)


def _w_in_layout_kernel(wt_ref, o_ref):
    head = wt_ref[:OFF_KR, :].T
    b = wt_ref[OFF_KR:OFF_KR + LANES, :].T
    tail = wt_ref[OFF_KR + QK_ROPE:, :].T
    lane = lax.broadcasted_iota(jnp.int32, b.shape, 1)
    half = QK_ROPE // 2
    kr = jnp.where(lane < QK_ROPE, b,
                   jnp.where(lane < QK_ROPE + half, -pltpu.roll(b, half, axis=1),
                             pltpu.roll(b, QK_ROPE + half, axis=1)))
    o_ref[...] = jnp.concatenate([head, kr, tail], axis=1).astype(_bf16)


def _prep_w_in(w_in):
    depth, rows, width = w_in.shape
    tr = W_IN_LAYOUT_ROWS
    return pl.pallas_call(
        _w_in_layout_kernel,
        grid=(depth, rows // tr),
        in_specs=[pl.BlockSpec((None, width, tr), lambda l, r: (l, 0, r))],
        out_specs=pl.BlockSpec((None, tr, IN_WIDTH_PAD), lambda l, r: (l, r, 0)),
        out_shape=jax.ShapeDtypeStruct((depth, rows, IN_WIDTH_PAD), _bf16),
        compiler_params=_params(("arbitrary", "arbitrary")),
        name="w_in_layout",
    )(jnp.swapaxes(w_in, 1, 2))


def _prep_w_uq_t(w_uq):
    scale = QK_HEAD ** -0.5 * LOG2E
    w = (w_uq * scale).astype(_bf16)
    pe = w[..., QK_NOPE:]
    w = jnp.concatenate([w[..., :QK_NOPE], pe, _rotated(pe)], axis=-1)
    return jnp.swapaxes(w.reshape(w.shape[0], Q_LORA, N_HEADS * QK_PAD), 1, 2)


def kernel(x, p, positions, norm_mix, w_in, w_pool, pool_scale, q_norm, kv_norm, w_uq, w_ukv, w_a, w_b, w_o,
           norm_ffn, w_gate, w_up, w_down, norm_ple, w_ple_gate, w_ple, final_norm):
    batch, seq, _ = x.shape
    depth = w_in.shape[0]
    t = batch * seq
    assert seq % ROW_TILE == 0 and seq % ATTN_TILE == 0 and ATTN_TILE % (2 * ATTN_BLOCK) == 0 and ATTN_BLOCK % ATTN_CHUNK == 0
    assert t % ROPE_TILE == 0 and N_HEADS % Q_HEAD_GROUP == 0

    inv_freq = 1.0 / (ROPE_THETA ** (jnp.arange(0, QK_ROPE, 2, dtype=_f32) / QK_ROPE))
    tables = _rope_tables(positions.astype(_f32).reshape(1, t), inv_freq[:, None])

    def vec(a):
        return a[:, None, :]

    def cast(a):
        return a.astype(_bf16)

    g_mix, g_q, g_kv, g_ffn, g_ple = vec(norm_mix), vec(q_norm), vec(kv_norm), vec(norm_ffn), vec(norm_ple)
    ps = vec(pool_scale)
    w_in_p = _prep_w_in(w_in)
    w_pool_b = cast(w_pool)
    w_uqt = _prep_w_uq_t(w_uq)
    w_ukv_b = cast(w_ukv)
    w_uk = w_ukv_b[..., :QK_NOPE].reshape(depth, KV_LORA, N_HEADS * QK_NOPE)
    w_vt = jnp.swapaxes(w_ukv_b[..., QK_NOPE:].reshape(depth, KV_LORA, N_HEADS * V_HEAD), 1, 2)
    w_a_b, w_b_b, w_o_b = cast(w_a), cast(w_b), cast(w_o)
    w_gate_b, w_up_b, w_down_b = cast(w_gate), cast(w_up), cast(w_down)
    w_pg_b, w_ple_b = cast(w_ple_gate), cast(w_ple)
    gl = final_norm.reshape(1, D_MODEL)
    p3 = p.reshape(depth, t, PLE_DIM)

    x2 = x.reshape(t, D_MODEL)
    for i in range(depth):
        mixed, qt, k, vt, sga, sgb = _front(x2, i, g_mix, w_in_p, w_pool_b, ps, g_q, g_kv, w_uqt, w_uk, w_vt,
                                            tables, seq)
        attn = _attention(qt, k, vt, batch, seq)
        back_weights = (w_a_b, w_b_b, w_o_b, g_ffn, w_gate_b, w_up_b, w_down_b, g_ple, w_pg_b, w_ple_b)
        x2 = _back(x2, attn, mixed, sga, sgb, p3, i, back_weights, gl, final=(i == depth - 1))
    return x2.reshape(batch, seq, D_MODEL)
```

```python
import functools

import jax
import jax.numpy as jnp
from jax import lax
from jax.experimental import pallas as pl
from jax.experimental.pallas import tpu as pltpu

D_MODEL = 1024
PLE_DIM = 256
POOL_WINDOWS = (2, 4, 8, 16)
POOL_GROUP = 128
POOL_WIDTH = 512
N_HEADS = 8
Q_LORA = 512
KV_LORA = 256
QK_NOPE = 128
QK_ROPE = 64
QK_HEAD = QK_NOPE + QK_ROPE
V_HEAD = 128
D_FF = 2816
ROPE_THETA = 10000.0
EPS = 1e-6

LANES = 128
QK_PAD = 2 * LANES
POOL_HALO = 16

OFF_U = 0
OFF_CQ = OFF_U + POOL_WIDTH
OFF_CKV = OFF_CQ + Q_LORA
OFF_KR = OFF_CKV + KV_LORA
OFF_GA = OFF_KR + LANES
OFF_GB = OFF_GA + D_MODEL
IN_WIDTH_PAD = OFF_GB + D_MODEL

FRONT_TILE = 1024
BACK_TILE = 512
ROPE_TILE = 2048
W_IN_LAYOUT_ROWS = 256
Q_HEAD_GROUP = 4
ATTN_TILE = 4096
ATTN_BLOCK = 512
ATTN_BLOCKS_PER_TRIP = 4
ATTN_CHUNK = 256
SUM_ROWS = 16
FF_CHUNK = 256
VMEM_LIMIT = 60 * 1024 * 1024

MASK_VALUE = -1e30
LOG2E = 1.4426950408889634

_f32 = jnp.float32
_bf16 = jnp.bfloat16
_NT = (((1,), (1,)), ((), ()))


def _dot(a, b):
    return jnp.dot(a, b, preferred_element_type=_f32)


def _dot_nt(a, b):
    return lax.dot_general(a, b, _NT, preferred_element_type=_f32)


def _rms(x, g):
    return x * lax.rsqrt(jnp.mean(x * x, axis=-1, keepdims=True) + EPS) * g


def _resident(shape):
    nd = len(shape)
    return pl.BlockSpec(shape, lambda *_: (0,) * nd, pipeline_mode=pl.Buffered(1))


def _layer(arr, layer):
    nd = arr.ndim - 1
    return pl.BlockSpec((None,) + arr.shape[1:], lambda *_: (layer,) + (0,) * nd, pipeline_mode=pl.Buffered(1))


def _params(sem):
    return pltpu.CompilerParams(dimension_semantics=sem, vmem_limit_bytes=VMEM_LIMIT)


def _rope_table_kernel(pos_ref, inv_ref, cos_ref, sin_ref, cost_ref, sint_ref):
    ang = inv_ref[...] * pos_ref[...]
    c = jnp.cos(ang)
    s = jnp.sin(ang)
    zeros = jnp.zeros((LANES - QK_ROPE, ang.shape[1]), _f32)
    ct = jnp.concatenate([c, c], axis=0)
    st = jnp.concatenate([s, s], axis=0)
    cost_ref[...] = ct
    sint_ref[...] = st
    cos_ref[...] = jnp.concatenate([ct, zeros], axis=0).T
    sin_ref[...] = jnp.concatenate([st, zeros], axis=0).T


def _rope_tables(pos_row, inv_col):
    t = pos_row.shape[1]
    tm = ROPE_TILE
    row = pl.BlockSpec((tm, LANES), lambda i: (i, 0))
    col = pl.BlockSpec((QK_ROPE, tm), lambda i: (0, i))
    return pl.pallas_call(
        _rope_table_kernel,
        grid=(t // tm,),
        in_specs=[pl.BlockSpec((1, tm), lambda i: (0, i)), _resident(inv_col.shape)],
        out_specs=[row, row, col, col],
        out_shape=[jax.ShapeDtypeStruct((t, LANES), _f32)] * 2 + [jax.ShapeDtypeStruct((QK_ROPE, t), _f32)] * 2,
        compiler_params=_params(("arbitrary",)),
        name="rope_tables",
    )(pos_row, inv_col)


def _front_kernel(x_ref, g_ref, w_in_ref, w_pool_ref, ps_ref, qg_ref, kvg_ref, w_uqt_ref, w_uk_ref, w_vt_ref,
                  cos_ref, sin_ref, cost_ref, sint_ref,
                  mixed_ref, qt_ref, k_ref, vt_ref, sga_ref, sgb_ref,
                  carry_ref, *, tm, seq):
    i = pl.program_id(0)
    t0 = (i * tm) % seq
    h = _rms(x_ref[...], g_ref[...]).astype(_bf16)

    u = _dot(h, w_in_ref[:, OFF_U:OFF_U + POOL_WIDTH])

    @pl.when(t0 == 0)
    def _():
        carry_ref[...] = jnp.zeros_like(carry_ref)

    ext = jnp.concatenate([carry_ref[...], u], axis=0)
    carry_ref[...] = u[tm - POOL_HALO:, :]
    sums = []
    cur = ext
    step = 1
    for _ in POOL_WINDOWS:
        cur = cur + pltpu.roll(cur, step, axis=0)
        sums.append(cur[POOL_HALO:, :POOL_GROUP])
        cur = cur[:, POOL_GROUP:]
        step *= 2
    t_idx = t0 + lax.broadcasted_iota(jnp.int32, (tm, POOL_GROUP), 0)
    for g, w in enumerate(POOL_WINDOWS):
        cols = slice(g * POOL_GROUP, (g + 1) * POOL_GROUP)
        cnt = jnp.minimum(t_idx + 1, w).astype(_f32)
        pooled = sums[g] / cnt - u[:, cols]
        mixed = _dot(pooled.astype(_bf16), w_pool_ref[g]) * ps_ref[:, cols]
        mixed_ref[:, cols] = mixed.astype(_bf16)

    cq = _dot(h, w_in_ref[:, OFF_CQ:OFF_CQ + Q_LORA])
    cqn = _rms(cq, qg_ref[...]).astype(_bf16)
    half = QK_ROPE // 2
    cos_h = cost_ref[:half, :]
    sin_h = sint_ref[:half, :]
    rows_per_group = Q_HEAD_GROUP * QK_HEAD
    for g0 in range(0, N_HEADS, Q_HEAD_GROUP):
        qt = _dot_nt(w_uqt_ref[g0 * QK_HEAD:g0 * QK_HEAD + rows_per_group, :], cqn)
        for j in range(Q_HEAD_GROUP):
            r = j * QK_HEAD
            o = (g0 + j) * QK_PAD
            x1 = qt[r + QK_NOPE:r + QK_NOPE + half]
            x2 = qt[r + QK_NOPE + half:r + QK_HEAD]
            qt_ref[o:o + QK_NOPE, :] = qt[r:r + QK_NOPE].astype(_bf16)
            qt_ref[o + QK_NOPE:o + QK_NOPE + half, :] = (x1 * cos_h - x2 * sin_h).astype(_bf16)
            qt_ref[o + QK_NOPE + half:o + QK_HEAD, :] = (x2 * cos_h + x1 * sin_h).astype(_bf16)
            qt_ref[o + QK_HEAD:o + QK_PAD, :] = jnp.zeros((QK_PAD - QK_HEAD, tm), _bf16)

    ckv = _dot(h, w_in_ref[:, OFF_CKV:OFF_CKV + KV_LORA])
    ckvn = _rms(ckv, kvg_ref[...]).astype(_bf16)
    kr = _dot(h, w_in_ref[:, OFF_KR:OFF_KR + LANES])
    k_pe = (kr * cos_ref[...] + pltpu.roll(kr, QK_ROPE, axis=1) * sin_ref[...]).astype(_bf16)
    k_nope = _dot(ckvn, w_uk_ref[...]).astype(_bf16)
    for hd in range(N_HEADS):
        k_ref[:, hd * QK_PAD:hd * QK_PAD + QK_NOPE] = k_nope[:, hd * QK_NOPE:(hd + 1) * QK_NOPE]
        k_ref[:, hd * QK_PAD + QK_NOPE:(hd + 1) * QK_PAD] = k_pe
    vt_ref[...] = _dot_nt(w_vt_ref[...], ckvn).astype(_bf16)

    sga_ref[...] = jax.nn.sigmoid(_dot(h, w_in_ref[:, OFF_GA:OFF_GA + D_MODEL])).astype(_bf16)
    sgb_ref[...] = jax.nn.sigmoid(_dot(h, w_in_ref[:, OFF_GB:OFF_GB + D_MODEL])).astype(_bf16)


def _front(x2, layer, g, w_in, w_pool, ps, qg, kvg, w_uqt, w_uk, w_vt, tables, seq):
    t = x2.shape[0]
    tm = FRONT_TILE
    cos_t, sin_t, cos_tt, sin_tt = tables

    def row(width):
        return pl.BlockSpec((tm, width), lambda i: (i, 0))

    def col(height):
        return pl.BlockSpec((height, tm), lambda i: (0, i))

    def row_out(width):
        return jax.ShapeDtypeStruct((t, width), _bf16)

    def col_out(height):
        return jax.ShapeDtypeStruct((height, t), _bf16)

    weights = (g, w_in, w_pool, ps, qg, kvg, w_uqt, w_uk, w_vt)
    return pl.pallas_call(
        functools.partial(_front_kernel, tm=tm, seq=seq),
        grid=(t // tm,),
        in_specs=[row(D_MODEL)] + [_layer(w, layer) for w in weights]
                 + [row(LANES), row(LANES), col(QK_ROPE), col(QK_ROPE)],
        out_specs=[row(POOL_WIDTH), col(N_HEADS * QK_PAD), row(N_HEADS * QK_PAD), col(N_HEADS * V_HEAD),
                   row(D_MODEL), row(D_MODEL)],
        out_shape=[row_out(POOL_WIDTH), col_out(N_HEADS * QK_PAD), row_out(N_HEADS * QK_PAD),
                   col_out(N_HEADS * V_HEAD), row_out(D_MODEL), row_out(D_MODEL)],
        scratch_shapes=[pltpu.VMEM((POOL_HALO, POOL_WIDTH), _f32)],
        compiler_params=_params(("arbitrary",)),
        name="front",
    )(x2, *weights, cos_t, sin_t, cos_tt, sin_tt)


def _attn_kernel(qt_ref, k_ref, vt_ref, o_ref, s0_scr, s1_scr, m_scr, acc_scr, *, tile, block,
                 chunk, per_trip):
    i = pl.program_id(2)
    nb = tile // block

    def scores(q_col0, nq, blk):
        start = pl.multiple_of(blk * block, block)
        return _dot(k_ref[pl.ds(start, block), :], qt_ref[:, pl.ds(q_col0, nq)])

    def values(blk):
        vt = vt_ref[:, pl.ds(pl.multiple_of(blk * block, block), block)]
        return jnp.concatenate([vt, jnp.ones((SUM_ROWS, block), _bf16)], axis=0)

    def absorb(s_ref, s_col0, col0, ncols, vt, key0=None):
        for c in range(ncols // chunk):
            cols = pl.ds(col0 + c * chunk, chunk)
            if key0 is None:
                s = s_ref[:, pl.ds(s_col0 + c * chunk, chunk)]
                vc = vt
            else:
                nkeys = col0 + (c + 1) * chunk - key0
                s = s_ref[pl.ds(0, nkeys), pl.ds(s_col0 + c * chunk, chunk)]
                vc = vt[:, :nkeys]
                key_id = key0 + lax.broadcasted_iota(jnp.int32, s.shape, 0)
                q_id = col0 + c * chunk + lax.broadcasted_iota(jnp.int32, s.shape, 1)
                s = jnp.where(key_id <= q_id, s, MASK_VALUE)
            m_old = m_scr[:, cols]
            m_new = jnp.maximum(m_old, jnp.max(s, axis=0, keepdims=True))
            alpha = jnp.exp2(m_old - m_new)
            p = jnp.exp2(s - m_new)
            m_scr[:, cols] = m_new
            acc_scr[:, cols] = alpha * acc_scr[:, cols] + _dot(vc, p.astype(_bf16))

    def finish(col0, ncols):
        a = acc_scr[:, pl.ds(col0, ncols)]
        o_ref[pl.ds(col0, ncols), :] = (a[:V_HEAD] / a[V_HEAD:V_HEAD + 1]).T.astype(_bf16)

    m_scr[...] = jnp.full(m_scr.shape, MASK_VALUE, _f32)
    acc_scr[...] = jnp.zeros(acc_scr.shape, _f32)
    s0_scr[...] = scores(0, tile, 0)

    bufs = (s0_scr, s1_scr)

    def body(t, carry):
        for kb in range(per_trip):
            blk = per_trip * t + kb
            bufs[(kb + 1) % 2][...] = scores(0, tile, blk + 1)
            absorb(bufs[kb % 2], 0, 0, tile, values(blk))
        return carry

    lax.fori_loop(0, i * (nb // per_trip), body, 0)

    for kb in range(nb):
        cur, nxt = bufs[kb % 2], bufs[(kb + 1) % 2]
        q0 = kb * block
        if kb + 1 < nb:
            nxt[:, pl.ds(0, tile - q0 - block)] = scores(q0 + block, tile - q0 - block, nb * i + kb + 1)
        vt = values(nb * i + kb)
        absorb(cur, 0, q0, block, vt, key0=q0)
        finish(q0, block)
        if kb + 1 < nb:
            absorb(cur, block, q0 + block, tile - q0 - block, vt)


def _attention(qt, k, vt, batch, seq):
    tq = ATTN_TILE
    nq = seq // tq
    stat = pltpu.VMEM((1, tq), _f32)
    score = pltpu.VMEM((ATTN_BLOCK, tq), _f32)
    return pl.pallas_call(
        functools.partial(_attn_kernel, tile=ATTN_TILE, block=ATTN_BLOCK, chunk=ATTN_CHUNK,
                          per_trip=ATTN_BLOCKS_PER_TRIP),
        grid=(batch, N_HEADS, nq),
        in_specs=[pl.BlockSpec((QK_PAD, tq), lambda b, h, i: (h, b * nq + i)),
                  pl.BlockSpec((seq, QK_PAD), lambda b, h, i: (b, h)),
                  pl.BlockSpec((V_HEAD, seq), lambda b, h, i: (h, b))],
        out_specs=pl.BlockSpec((tq, V_HEAD), lambda b, h, i: (b * nq + i, h)),
        out_shape=jax.ShapeDtypeStruct((batch * seq, N_HEADS * V_HEAD), _bf16),
        scratch_shapes=[score, score, stat, pltpu.VMEM((V_HEAD + SUM_ROWS, tq), _f32)],
        compiler_params=_params(("arbitrary", "arbitrary", "arbitrary")),
        name="attention",
    )(qt, k, vt)


def _back_kernel(x_ref, attn_ref, mixed_ref, sga_ref, sgb_ref, p_ref, w_a_ref, w_b_ref, w_o_ref,
                 gf_ref, w_gate_ref, w_up_ref, w_down_ref, gp_ref, w_pg_ref, w_ple_ref, gl_ref,
                 o_ref, *, final):
    y_a = _dot(mixed_ref[...], w_a_ref[...])
    y_b = _dot(attn_ref[...], w_b_ref[...])
    merged = sga_ref[...].astype(_f32) * y_a + sgb_ref[...].astype(_f32) * y_b
    x = x_ref[...] + _dot(merged.astype(_bf16), w_o_ref[...])
    h = _rms(x, gf_ref[...]).astype(_bf16)
    acc = None
    for c0 in range(0, D_FF, FF_CHUNK):
        cols = slice(c0, min(c0 + FF_CHUNK, D_FF))
        gate = _dot(h, w_gate_ref[:, cols])
        up = _dot(h, w_up_ref[:, cols])
        act = (gate * jax.nn.sigmoid(gate) * up).astype(_bf16)
        part = _dot(act, w_down_ref[cols, :])
        acc = part if acc is None else acc + part
    x = x + acc
    hp = _rms(x, gp_ref[...]).astype(_bf16)
    gate = jax.nn.sigmoid(_dot(hp, w_pg_ref[...]))
    x = x + gate * _dot(p_ref[...].astype(_bf16), w_ple_ref[...])
    if final:
        x = _rms(x, gl_ref[...])
    o_ref[...] = x


def _back(x2, attn, mixed, sga, sgb, p3, layer, weights, gl, final):
    t = x2.shape[0]
    tm = BACK_TILE

    def row(width):
        return pl.BlockSpec((tm, width), lambda i: (i, 0))

    return pl.pallas_call(
        functools.partial(_back_kernel, final=final),
        grid=(t // tm,),
        in_specs=[row(D_MODEL), row(D_MODEL), row(POOL_WIDTH), row(D_MODEL), row(D_MODEL),
                  pl.BlockSpec((None, tm, PLE_DIM), lambda i: (layer, i, 0))]
                 + [_layer(w, layer) for w in weights] + [_resident(gl.shape)],
        out_specs=row(D_MODEL),
        out_shape=jax.ShapeDtypeStruct((t, D_MODEL), _f32),
        compiler_params=_params(("arbitrary",)),
        name="back",
    )(x2, attn, mixed, sga, sgb, p3, *weights, gl)


def _w_in_layout_kernel(wt_ref, o_ref):
    head = wt_ref[:OFF_KR, :].T
    b = wt_ref[OFF_KR:OFF_KR + LANES, :].T
    tail = wt_ref[OFF_KR + QK_ROPE:, :].T
    lane = lax.broadcasted_iota(jnp.int32, b.shape, 1)
    half = QK_ROPE // 2
    kr = jnp.where(lane < QK_ROPE, b,
                   jnp.where(lane < QK_ROPE + half, -pltpu.roll(b, half, axis=1),
                             pltpu.roll(b, QK_ROPE + half, axis=1)))
    o_ref[...] = jnp.concatenate([head, kr, tail], axis=1).astype(_bf16)


def _prep_w_in(w_in):
    depth, rows, width = w_in.shape
    tr = W_IN_LAYOUT_ROWS
    return pl.pallas_call(
        _w_in_layout_kernel,
        grid=(depth, rows // tr),
        in_specs=[pl.BlockSpec((None, width, tr), lambda l, r: (l, 0, r))],
        out_specs=pl.BlockSpec((None, tr, IN_WIDTH_PAD), lambda l, r: (l, r, 0)),
        out_shape=jax.ShapeDtypeStruct((depth, rows, IN_WIDTH_PAD), _bf16),
        compiler_params=_params(("arbitrary", "arbitrary")),
        name="w_in_layout",
    )(jnp.swapaxes(w_in, 1, 2))


def _prep_w_uq_t(w_uq):
    scale = QK_HEAD ** -0.5 * LOG2E
    w = (w_uq * scale).astype(_bf16)
    return jnp.swapaxes(w.reshape(w.shape[0], Q_LORA, N_HEADS * QK_HEAD), 1, 2)


def kernel(x, p, positions, norm_mix, w_in, w_pool, pool_scale, q_norm, kv_norm, w_uq, w_ukv, w_a, w_b, w_o,
           norm_ffn, w_gate, w_up, w_down, norm_ple, w_ple_gate, w_ple, final_norm):
    batch, seq, _ = x.shape
    depth = w_in.shape[0]
    t = batch * seq
    assert seq % FRONT_TILE == 0 and t % BACK_TILE == 0 and seq % ATTN_TILE == 0
    assert ATTN_TILE % (ATTN_BLOCKS_PER_TRIP * ATTN_BLOCK) == 0 and ATTN_BLOCK % ATTN_CHUNK == 0
    assert t % ROPE_TILE == 0 and N_HEADS % Q_HEAD_GROUP == 0

    inv_freq = 1.0 / (ROPE_THETA ** (jnp.arange(0, QK_ROPE, 2, dtype=_f32) / QK_ROPE))
    tables = _rope_tables(positions.astype(_f32).reshape(1, t), inv_freq[:, None])

    def vec(a):
        return a[:, None, :]

    def cast(a):
        return a.astype(_bf16)

    g_mix, g_q, g_kv, g_ffn, g_ple = vec(norm_mix), vec(q_norm), vec(kv_norm), vec(norm_ffn), vec(norm_ple)
    ps = vec(pool_scale)
    w_in_p = _prep_w_in(w_in)
    w_pool_b = cast(w_pool)
    w_uqt = _prep_w_uq_t(w_uq)
    w_ukv_b = cast(w_ukv)
    w_uk = w_ukv_b[..., :QK_NOPE].reshape(depth, KV_LORA, N_HEADS * QK_NOPE)
    w_vt = jnp.swapaxes(w_ukv_b[..., QK_NOPE:].reshape(depth, KV_LORA, N_HEADS * V_HEAD), 1, 2)
    w_a_b, w_b_b, w_o_b = cast(w_a), cast(w_b), cast(w_o)
    w_gate_b, w_up_b, w_down_b = cast(w_gate), cast(w_up), cast(w_down)
    w_pg_b, w_ple_b = cast(w_ple_gate), cast(w_ple)
    gl = final_norm.reshape(1, D_MODEL)
    p3 = p.reshape(depth, t, PLE_DIM)

    x2 = x.reshape(t, D_MODEL)
    for i in range(depth):
        mixed, qt, k, vt, sga, sgb = _front(x2, i, g_mix, w_in_p, w_pool_b, ps, g_q, g_kv, w_uqt, w_uk, w_vt,
                                            tables, seq)
        attn = _attention(qt, k, vt, batch, seq)
        back_weights = (w_a_b, w_b_b, w_o_b, g_ffn, w_gate_b, w_up_b, w_down_b, g_ple, w_pg_b, w_ple_b)
        x2 = _back(x2, attn, mixed, sga, sgb, p3, i, back_weights, gl, final=(i == depth - 1))
    return x2.reshape(batch, seq, D_MODEL)
```

```python
import functools

import jax
import jax.numpy as jnp
from jax import lax
from jax.experimental import pallas as pl
from jax.experimental.pallas import tpu as pltpu

D_MODEL = 1024
PLE_DIM = 256
POOL_WINDOWS = (2, 4, 8, 16)
POOL_GROUP = 128
POOL_WIDTH = 512
N_HEADS = 8
Q_LORA = 512
KV_LORA = 256
QK_NOPE = 128
QK_ROPE = 64
QK_HEAD = QK_NOPE + QK_ROPE
V_HEAD = 128
D_FF = 2816
ROPE_THETA = 10000.0
EPS = 1e-6

LANES = 128
QK_PAD = 2 * LANES
POOL_HALO = 16

OFF_U = 0
OFF_CQ = OFF_U + POOL_WIDTH
OFF_CKV = OFF_CQ + Q_LORA
OFF_KR = OFF_CKV + KV_LORA
OFF_GA = OFF_KR + LANES
OFF_GB = OFF_GA + D_MODEL
IN_WIDTH_PAD = OFF_GB + D_MODEL

FRONT_TILE = 1024
BACK_TILE = 512
ROPE_TILE = 8192
W_IN_LAYOUT_ROWS = 256
Q_HEAD_GROUP = 4
ATTN_TILE = 4096
ATTN_BLOCK = 512
ATTN_BLOCKS_PER_TRIP = 4
ATTN_CHUNK = 256
SUM_ROWS = 16
FF_CHUNK = 256
VMEM_LIMIT = 60 * 1024 * 1024

MASK_VALUE = -1e30
LOG2E = 1.4426950408889634

_f32 = jnp.float32
_bf16 = jnp.bfloat16
_NT = (((1,), (1,)), ((), ()))


def _dot(a, b):
    return jnp.dot(a, b, preferred_element_type=_f32)


def _dot_nt(a, b):
    return lax.dot_general(a, b, _NT, preferred_element_type=_f32)


def _rms(x, g):
    return x * lax.rsqrt(jnp.mean(x * x, axis=-1, keepdims=True) + EPS) * g


def _resident(shape):
    nd = len(shape)
    return pl.BlockSpec(shape, lambda *_: (0,) * nd, pipeline_mode=pl.Buffered(1))


def _layer(arr, layer):
    nd = arr.ndim - 1
    return pl.BlockSpec((None,) + arr.shape[1:], lambda *_: (layer,) + (0,) * nd, pipeline_mode=pl.Buffered(1))


def _params(sem):
    return pltpu.CompilerParams(dimension_semantics=sem, vmem_limit_bytes=VMEM_LIMIT)


def _rope_table_kernel(pos_ref, inv_ref, cos_ref, sin_ref, cost_ref, sint_ref):
    ang = inv_ref[...] * pos_ref[...]
    c = jnp.cos(ang)
    s = jnp.sin(ang)
    zeros = jnp.zeros((LANES - QK_ROPE, ang.shape[1]), _f32)
    ct = jnp.concatenate([c, c], axis=0)
    st = jnp.concatenate([s, s], axis=0)
    cost_ref[...] = ct
    sint_ref[...] = st
    cos_ref[...] = jnp.concatenate([ct, zeros], axis=0).T
    sin_ref[...] = jnp.concatenate([st, zeros], axis=0).T


def _rope_tables(pos_row, inv_col):
    t = pos_row.shape[1]
    tm = ROPE_TILE
    row = pl.BlockSpec((tm, LANES), lambda i: (i, 0))
    col = pl.BlockSpec((QK_ROPE, tm), lambda i: (0, i))
    return pl.pallas_call(
        _rope_table_kernel,
        grid=(t // tm,),
        in_specs=[pl.BlockSpec((1, tm), lambda i: (0, i)), _resident(inv_col.shape)],
        out_specs=[row, row, col, col],
        out_shape=[jax.ShapeDtypeStruct((t, LANES), _f32)] * 2 + [jax.ShapeDtypeStruct((QK_ROPE, t), _f32)] * 2,
        compiler_params=_params(("arbitrary",)),
        name="rope_tables",
    )(pos_row, inv_col)


def _front_kernel(x_ref, g_ref, w_in_ref, w_pool_ref, ps_ref, qg_ref, kvg_ref, w_uqt_ref, w_uk_ref, w_vt_ref,
                  cos_ref, sin_ref, cost_ref, sint_ref,
                  mixed_ref, qt_ref, k_ref, vt_ref, sga_ref, sgb_ref,
                  carry_ref, *, tm, seq):
    i = pl.program_id(0)
    t0 = (i * tm) % seq

    @pl.when(t0 == 0)
    def _():
        carry_ref[...] = jnp.zeros_like(carry_ref)

    h = _rms(x_ref[...], g_ref[...]).astype(_bf16)

    sga_ref[...] = jax.nn.sigmoid(_dot(h, w_in_ref[:, OFF_GA:OFF_GA + D_MODEL])).astype(_bf16)
    sgb_ref[...] = jax.nn.sigmoid(_dot(h, w_in_ref[:, OFF_GB:OFF_GB + D_MODEL])).astype(_bf16)

    u = _dot(h, w_in_ref[:, OFF_U:OFF_U + POOL_WIDTH])
    ext = jnp.concatenate([carry_ref[...], u], axis=0)
    carry_ref[...] = u[tm - POOL_HALO:, :]
    sums = []
    cur = ext
    step = 1
    for _ in POOL_WINDOWS:
        cur = cur + pltpu.roll(cur, step, axis=0)
        sums.append(cur[POOL_HALO:, :POOL_GROUP])
        cur = cur[:, POOL_GROUP:]
        step *= 2
    t_idx = t0 + lax.broadcasted_iota(jnp.int32, (tm, POOL_GROUP), 0)
    for g, w in enumerate(POOL_WINDOWS):
        cols = slice(g * POOL_GROUP, (g + 1) * POOL_GROUP)
        cnt = jnp.minimum(t_idx + 1, w).astype(_f32)
        pooled = sums[g] / cnt - u[:, cols]
        mixed = _dot(pooled.astype(_bf16), w_pool_ref[g]) * ps_ref[:, cols]
        mixed_ref[:, cols] = mixed.astype(_bf16)

    cq = _dot(h, w_in_ref[:, OFF_CQ:OFF_CQ + Q_LORA])
    cqn = _rms(cq, qg_ref[...]).astype(_bf16)
    half = QK_ROPE // 2
    cos_h = cost_ref[:half, :]
    sin_h = sint_ref[:half, :]
    rows_per_group = Q_HEAD_GROUP * QK_HEAD
    for g0 in range(0, N_HEADS, Q_HEAD_GROUP):
        qt = _dot_nt(w_uqt_ref[g0 * QK_HEAD:g0 * QK_HEAD + rows_per_group, :], cqn)
        for j in range(Q_HEAD_GROUP):
            r = j * QK_HEAD
            o = (g0 + j) * QK_PAD
            x1 = qt[r + QK_NOPE:r + QK_NOPE + half]
            x2 = qt[r + QK_NOPE + half:r + QK_HEAD]
            qt_ref[o:o + QK_NOPE, :] = qt[r:r + QK_NOPE].astype(_bf16)
            qt_ref[o + QK_NOPE:o + QK_NOPE + half, :] = (x1 * cos_h - x2 * sin_h).astype(_bf16)
            qt_ref[o + QK_NOPE + half:o + QK_HEAD, :] = (x2 * cos_h + x1 * sin_h).astype(_bf16)
            qt_ref[o + QK_HEAD:o + QK_PAD, :] = jnp.zeros((QK_PAD - QK_HEAD, tm), _bf16)

    ckv = _dot(h, w_in_ref[:, OFF_CKV:OFF_CKV + KV_LORA])
    ckvn = _rms(ckv, kvg_ref[...]).astype(_bf16)
    kr = _dot(h, w_in_ref[:, OFF_KR:OFF_KR + LANES])
    k_pe = (kr * cos_ref[...] + pltpu.roll(kr, QK_ROPE, axis=1) * sin_ref[...]).astype(_bf16)
    k_nope = _dot(ckvn, w_uk_ref[...]).astype(_bf16)
    for hd in range(N_HEADS):
        k_ref[:, hd * QK_PAD:hd * QK_PAD + QK_NOPE] = k_nope[:, hd * QK_NOPE:(hd + 1) * QK_NOPE]
        k_ref[:, hd * QK_PAD + QK_NOPE:(hd + 1) * QK_PAD] = k_pe
    vt_ref[...] = _dot_nt(w_vt_ref[...], ckvn).astype(_bf16)


def _front(x2, layer, g, w_in, w_pool, ps, qg, kvg, w_uqt, w_uk, w_vt, tables, seq):
    t = x2.shape[0]
    tm = FRONT_TILE
    cos_t, sin_t, cos_tt, sin_tt = tables

    def row(width):
        return pl.BlockSpec((tm, width), lambda i: (i, 0))

    def col(height):
        return pl.BlockSpec((height, tm), lambda i: (0, i))

    def row_out(width):
        return jax.ShapeDtypeStruct((t, width), _bf16)

    def col_out(height):
        return jax.ShapeDtypeStruct((height, t), _bf16)

    weights = (g, w_in, w_pool, ps, qg, kvg, w_uqt, w_uk, w_vt)
    return pl.pallas_call(
        functools.partial(_front_kernel, tm=tm, seq=seq),
        grid=(t // tm,),
        in_specs=[row(D_MODEL)] + [_layer(w, layer) for w in weights]
                 + [row(LANES), row(LANES), col(QK_ROPE), col(QK_ROPE)],
        out_specs=[row(POOL_WIDTH), col(N_HEADS * QK_PAD), row(N_HEADS * QK_PAD), col(N_HEADS * V_HEAD),
                   row(D_MODEL), row(D_MODEL)],
        out_shape=[row_out(POOL_WIDTH), col_out(N_HEADS * QK_PAD), row_out(N_HEADS * QK_PAD),
                   col_out(N_HEADS * V_HEAD), row_out(D_MODEL), row_out(D_MODEL)],
        scratch_shapes=[pltpu.VMEM((POOL_HALO, POOL_WIDTH), _f32)],
        compiler_params=_params(("arbitrary",)),
        name="front",
    )(x2, *weights, cos_t, sin_t, cos_tt, sin_tt)


def _attn_kernel(qt_ref, k_ref, vt_ref, o_ref, s0_scr, s1_scr, m_scr, acc_scr, *, tile, block,
                 chunk, per_trip):
    i = pl.program_id(2)
    nb = tile // block

    def scores(q_col0, nq, blk):
        start = pl.multiple_of(blk * block, block)
        return _dot(k_ref[pl.ds(start, block), :], qt_ref[:, pl.ds(q_col0, nq)])

    def values(blk):
        vt = vt_ref[:, pl.ds(pl.multiple_of(blk * block, block), block)]
        return jnp.concatenate([vt, jnp.ones((SUM_ROWS, block), _bf16)], axis=0)

    def absorb(s_ref, s_col0, col0, ncols, vt, key0=None):
        for c in range(ncols // chunk):
            cols = pl.ds(col0 + c * chunk, chunk)
            if key0 is None:
                s = s_ref[:, pl.ds(s_col0 + c * chunk, chunk)]
                vc = vt
            else:
                nkeys = col0 + (c + 1) * chunk - key0
                s = s_ref[pl.ds(0, nkeys), pl.ds(s_col0 + c * chunk, chunk)]
                vc = vt[:, :nkeys]
                key_id = key0 + lax.broadcasted_iota(jnp.int32, s.shape, 0)
                q_id = col0 + c * chunk + lax.broadcasted_iota(jnp.int32, s.shape, 1)
                s = jnp.where(key_id <= q_id, s, MASK_VALUE)
            m_old = m_scr[:, cols]
            m_new = jnp.maximum(m_old, jnp.max(s, axis=0, keepdims=True))
            alpha = jnp.exp2(m_old - m_new)
            p = jnp.exp2(s - m_new)
            m_scr[:, cols] = m_new
            acc_scr[:, cols] = alpha * acc_scr[:, cols] + _dot(vc, p.astype(_bf16))

    def finish(col0, ncols):
        a = acc_scr[:, pl.ds(col0, ncols)]
        o_ref[pl.ds(col0, ncols), :] = (a[:V_HEAD] / a[V_HEAD:V_HEAD + 1]).T.astype(_bf16)

    m_scr[...] = jnp.full(m_scr.shape, MASK_VALUE, _f32)
    acc_scr[...] = jnp.zeros(acc_scr.shape, _f32)
    s0_scr[...] = scores(0, tile, 0)

    bufs = (s0_scr, s1_scr)

    def body(t, carry):
        for kb in range(per_trip):
            blk = per_trip * t + kb
            bufs[(kb + 1) % 2][...] = scores(0, tile, blk + 1)
            absorb(bufs[kb % 2], 0, 0, tile, values(blk))
        return carry

    lax.fori_loop(0, i * (nb // per_trip), body, 0)

    for kb in range(nb):
        cur, nxt = bufs[kb % 2], bufs[(kb + 1) % 2]
        q0 = kb * block
        if kb + 1 < nb:
            nxt[:, pl.ds(0, tile - q0 - block)] = scores(q0 + block, tile - q0 - block, nb * i + kb + 1)
        vt = values(nb * i + kb)
        absorb(cur, 0, q0, block, vt, key0=q0)
        finish(q0, block)
        if kb + 1 < nb:
            absorb(cur, block, q0 + block, tile - q0 - block, vt)


def _attention(qt, k, vt, batch, seq):
    tq = ATTN_TILE
    nq = seq // tq
    stat = pltpu.VMEM((1, tq), _f32)
    score = pltpu.VMEM((ATTN_BLOCK, tq), _f32)
    return pl.pallas_call(
        functools.partial(_attn_kernel, tile=ATTN_TILE, block=ATTN_BLOCK, chunk=ATTN_CHUNK,
                          per_trip=ATTN_BLOCKS_PER_TRIP),
        grid=(batch, N_HEADS, nq),
        in_specs=[pl.BlockSpec((QK_PAD, tq), lambda b, h, i: (h, b * nq + i)),
                  pl.BlockSpec((seq, QK_PAD), lambda b, h, i: (b, h)),
                  pl.BlockSpec((V_HEAD, seq), lambda b, h, i: (h, b))],
        out_specs=pl.BlockSpec((tq, V_HEAD), lambda b, h, i: (b * nq + i, h)),
        out_shape=jax.ShapeDtypeStruct((batch * seq, N_HEADS * V_HEAD), _bf16),
        scratch_shapes=[score, score, stat, pltpu.VMEM((V_HEAD + SUM_ROWS, tq), _f32)],
        compiler_params=_params(("arbitrary", "arbitrary", "arbitrary")),
        name="attention",
    )(qt, k, vt)


def _back_kernel(x_ref, attn_ref, mixed_ref, sga_ref, sgb_ref, p_ref, w_a_ref, w_b_ref, w_o_ref,
                 gf_ref, w_gate_ref, w_up_ref, w_down_ref, gp_ref, w_pg_ref, w_ple_ref, gl_ref,
                 o_ref, *, final):
    y_a = _dot(mixed_ref[...], w_a_ref[...])
    y_b = _dot(attn_ref[...], w_b_ref[...])
    merged = sga_ref[...].astype(_f32) * y_a + sgb_ref[...].astype(_f32) * y_b
    x = x_ref[...] + _dot(merged.astype(_bf16), w_o_ref[...])
    h = _rms(x, gf_ref[...]).astype(_bf16)
    acc = None
    for c0 in range(0, D_FF, FF_CHUNK):
        cols = slice(c0, min(c0 + FF_CHUNK, D_FF))
        gate = _dot(h, w_gate_ref[:, cols])
        up = _dot(h, w_up_ref[:, cols])
        act = (gate * jax.nn.sigmoid(gate) * up).astype(_bf16)
        part = _dot(act, w_down_ref[cols, :])
        acc = part if acc is None else acc + part
    x = x + acc
    hp = _rms(x, gp_ref[...]).astype(_bf16)
    gate = jax.nn.sigmoid(_dot(hp, w_pg_ref[...]))
    x = x + gate * _dot(p_ref[...].astype(_bf16), w_ple_ref[...])
    if final:
        x = _rms(x, gl_ref[...])
    o_ref[...] = x


def _back(x2, attn, mixed, sga, sgb, p3, layer, weights, gl, final):
    t = x2.shape[0]
    tm = BACK_TILE

    def row(width):
        return pl.BlockSpec((tm, width), lambda i: (i, 0))

    return pl.pallas_call(
        functools.partial(_back_kernel, final=final),
        grid=(t // tm,),
        in_specs=[row(D_MODEL), row(D_MODEL), row(POOL_WIDTH), row(D_MODEL), row(D_MODEL),
                  pl.BlockSpec((None, tm, PLE_DIM), lambda i: (layer, i, 0))]
                 + [_layer(w, layer) for w in weights] + [_resident(gl.shape)],
        out_specs=row(D_MODEL),
        out_shape=jax.ShapeDtypeStruct((t, D_MODEL), _f32),
        compiler_params=_params(("arbitrary",)),
        name="back",
    )(x2, attn, mixed, sga, sgb, p3, *weights, gl)


def _w_in_layout_kernel(wt_ref, o_ref):
    head = wt_ref[:OFF_KR, :].T
    b = wt_ref[OFF_KR:OFF_KR + LANES, :].T
    tail = wt_ref[OFF_KR + QK_ROPE:, :].T
    lane = lax.broadcasted_iota(jnp.int32, b.shape, 1)
    half = QK_ROPE // 2
    kr = jnp.where(lane < QK_ROPE, b,
                   jnp.where(lane < QK_ROPE + half, -pltpu.roll(b, half, axis=1),
                             pltpu.roll(b, QK_ROPE + half, axis=1)))
    o_ref[...] = jnp.concatenate([head, kr, tail], axis=1).astype(_bf16)


def _prep_w_in(w_in):
    depth, rows, width = w_in.shape
    tr = W_IN_LAYOUT_ROWS
    return pl.pallas_call(
        _w_in_layout_kernel,
        grid=(depth, rows // tr),
        in_specs=[pl.BlockSpec((None, width, tr), lambda l, r: (l, 0, r))],
        out_specs=pl.BlockSpec((None, tr, IN_WIDTH_PAD), lambda l, r: (l, r, 0)),
        out_shape=jax.ShapeDtypeStruct((depth, rows, IN_WIDTH_PAD), _bf16),
        compiler_params=_params(("arbitrary", "arbitrary")),
        name="w_in_layout",
    )(jnp.swapaxes(w_in, 1, 2))


def _prep_w_uq_t(w_uq):
    scale = QK_HEAD ** -0.5 * LOG2E
    w = (w_uq * scale).astype(_bf16)
    return jnp.swapaxes(w.reshape(w.shape[0], Q_LORA, N_HEADS * QK_HEAD), 1, 2)


def kernel(x, p, positions, norm_mix, w_in, w_pool, pool_scale, q_norm, kv_norm, w_uq, w_ukv, w_a, w_b, w_o,
           norm_ffn, w_gate, w_up, w_down, norm_ple, w_ple_gate, w_ple, final_norm):
    batch, seq, _ = x.shape
    depth = w_in.shape[0]
    t = batch * seq
    assert seq % FRONT_TILE == 0 and t % BACK_TILE == 0 and seq % ATTN_TILE == 0
    assert ATTN_TILE % (ATTN_BLOCKS_PER_TRIP * ATTN_BLOCK) == 0 and ATTN_BLOCK % ATTN_CHUNK == 0
    assert t % ROPE_TILE == 0 and N_HEADS % Q_HEAD_GROUP == 0

    inv_freq = 1.0 / (ROPE_THETA ** (jnp.arange(0, QK_ROPE, 2, dtype=_f32) / QK_ROPE))
    tables = _rope_tables(positions.astype(_f32).reshape(1, t), inv_freq[:, None])

    def vec(a):
        return a[:, None, :]

    def cast(a):
        return a.astype(_bf16)

    g_mix, g_q, g_kv, g_ffn, g_ple = vec(norm_mix), vec(q_norm), vec(kv_norm), vec(norm_ffn), vec(norm_ple)
    ps = vec(pool_scale)
    w_in_p = _prep_w_in(w_in)
    w_pool_b = cast(w_pool)
    w_uqt = _prep_w_uq_t(w_uq)
    w_ukv_b = cast(w_ukv)
    w_uk = w_ukv_b[..., :QK_NOPE].reshape(depth, KV_LORA, N_HEADS * QK_NOPE)
    w_vt = jnp.swapaxes(w_ukv_b[..., QK_NOPE:].reshape(depth, KV_LORA, N_HEADS * V_HEAD), 1, 2)
    w_a_b, w_b_b, w_o_b = cast(w_a), cast(w_b), cast(w_o)
    w_gate_b, w_up_b, w_down_b = cast(w_gate), cast(w_up), cast(w_down)
    w_pg_b, w_ple_b = cast(w_ple_gate), cast(w_ple)
    gl = final_norm.reshape(1, D_MODEL)
    p3 = p.reshape(depth, t, PLE_DIM)

    x2 = x.reshape(t, D_MODEL)
    for i in range(depth):
        mixed, qt, k, vt, sga, sgb = _front(x2, i, g_mix, w_in_p, w_pool_b, ps, g_q, g_kv, w_uqt, w_uk, w_vt,
                                            tables, seq)
        attn = _attention(qt, k, vt, batch, seq)
        back_weights = (w_a_b, w_b_b, w_o_b, g_ffn, w_gate_b, w_up_b, w_down_b, g_ple, w_pg_b, w_ple_b)
        x2 = _back(x2, attn, mixed, sga, sgb, p3, i, back_weights, gl, final=(i == depth - 1))
    return x2.reshape(batch, seq, D_MODEL)
```

```python
import functools

import jax
import jax.numpy as jnp
from jax import lax
from jax.experimental import pallas as pl
from jax.experimental.pallas import tpu as pltpu

D_MODEL = 1024
PLE_DIM = 256
POOL_WINDOWS = (2, 4, 8, 16)
POOL_GROUP = 128
POOL_WIDTH = 512
N_HEADS = 8
Q_LORA = 512
KV_LORA = 256
QK_NOPE = 128
QK_ROPE = 64
QK_HEAD = QK_NOPE + QK_ROPE
V_HEAD = 128
D_FF = 2816
ROPE_THETA = 10000.0
EPS = 1e-6

LANES = 128
QK_PAD = 2 * LANES
POOL_HALO = 16

OFF_U = 0
OFF_CQ = OFF_U + POOL_WIDTH
OFF_CKV = OFF_CQ + Q_LORA
OFF_KR = OFF_CKV + KV_LORA
FRONT_WIDTH = OFF_KR + LANES
GATE_WIDTH = 2 * D_MODEL

FRONT_TILE = 1024
BACK_TILE = 512
ROPE_TILE = 2048
W_IN_LAYOUT_ROWS = 256
Q_HEAD_GROUP = 4
ATTN_TILE = 4096
ATTN_BLOCK = 512
ATTN_BLOCKS_PER_TRIP = 4
ATTN_CHUNK = 256
SUM_ROWS = 16
FF_CHUNK = 256
VMEM_LIMIT = 60 * 1024 * 1024

MASK_VALUE = -1e30
LOG2E = 1.4426950408889634

_f32 = jnp.float32
_bf16 = jnp.bfloat16
_NT = (((1,), (1,)), ((), ()))


def _dot(a, b):
    return jnp.dot(a, b, preferred_element_type=_f32)


def _dot_nt(a, b):
    return lax.dot_general(a, b, _NT, preferred_element_type=_f32)


def _rms(x, g):
    return x * lax.rsqrt(jnp.mean(x * x, axis=-1, keepdims=True) + EPS) * g


def _resident(shape):
    nd = len(shape)
    return pl.BlockSpec(shape, lambda *_: (0,) * nd, pipeline_mode=pl.Buffered(1))


def _layer(arr, layer):
    nd = arr.ndim - 1
    return pl.BlockSpec((None,) + arr.shape[1:], lambda *_: (layer,) + (0,) * nd, pipeline_mode=pl.Buffered(1))


def _params(sem):
    return pltpu.CompilerParams(dimension_semantics=sem, vmem_limit_bytes=VMEM_LIMIT)


def _rope_table_kernel(pos_ref, inv_ref, cos_ref, sin_ref, cost_ref, sint_ref):
    ang = inv_ref[...] * pos_ref[...]
    c = jnp.cos(ang)
    s = jnp.sin(ang)
    zeros = jnp.zeros((LANES - QK_ROPE, ang.shape[1]), _f32)
    ct = jnp.concatenate([c, c], axis=0)
    st = jnp.concatenate([s, s], axis=0)
    cost_ref[...] = ct
    sint_ref[...] = st
    cos_ref[...] = jnp.concatenate([ct, zeros], axis=0).T
    sin_ref[...] = jnp.concatenate([st, zeros], axis=0).T


def _rope_tables(pos_row, inv_col):
    t = pos_row.shape[1]
    tm = ROPE_TILE
    row = pl.BlockSpec((tm, LANES), lambda i: (i, 0))
    col = pl.BlockSpec((QK_ROPE, tm), lambda i: (0, i))
    return pl.pallas_call(
        _rope_table_kernel,
        grid=(t // tm,),
        in_specs=[pl.BlockSpec((1, tm), lambda i: (0, i)), _resident(inv_col.shape)],
        out_specs=[row, row, col, col],
        out_shape=[jax.ShapeDtypeStruct((t, LANES), _f32)] * 2 + [jax.ShapeDtypeStruct((QK_ROPE, t), _f32)] * 2,
        compiler_params=_params(("arbitrary",)),
        name="rope_tables",
    )(pos_row, inv_col)


def _front_kernel(x_ref, g_ref, w_in_ref, w_pool_ref, ps_ref, qg_ref, kvg_ref, w_uqt_ref, w_uk_ref, w_vt_ref,
                  cos_ref, sin_ref, cost_ref, sint_ref,
                  mixed_ref, qt_ref, k_ref, vt_ref,
                  carry_ref, *, tm, seq):
    i = pl.program_id(0)
    t0 = (i * tm) % seq
    h = _rms(x_ref[...], g_ref[...]).astype(_bf16)

    u = _dot(h, w_in_ref[:, OFF_U:OFF_U + POOL_WIDTH])

    @pl.when(t0 == 0)
    def _():
        carry_ref[...] = jnp.zeros_like(carry_ref)

    ext = jnp.concatenate([carry_ref[...], u], axis=0)
    carry_ref[...] = u[tm - POOL_HALO:, :]
    sums = []
    cur = ext
    step = 1
    for _ in POOL_WINDOWS:
        cur = cur + pltpu.roll(cur, step, axis=0)
        sums.append(cur[POOL_HALO:, :POOL_GROUP])
        cur = cur[:, POOL_GROUP:]
        step *= 2
    t_idx = t0 + lax.broadcasted_iota(jnp.int32, (tm, POOL_GROUP), 0)
    for g, w in enumerate(POOL_WINDOWS):
        cols = slice(g * POOL_GROUP, (g + 1) * POOL_GROUP)
        cnt = jnp.minimum(t_idx + 1, w).astype(_f32)
        pooled = sums[g] / cnt - u[:, cols]
        mixed = _dot(pooled.astype(_bf16), w_pool_ref[g]) * ps_ref[:, cols]
        mixed_ref[:, cols] = mixed.astype(_bf16)

    cq = _dot(h, w_in_ref[:, OFF_CQ:OFF_CQ + Q_LORA])
    cqn = _rms(cq, qg_ref[...]).astype(_bf16)
    half = QK_ROPE // 2
    cos_h = cost_ref[:half, :]
    sin_h = sint_ref[:half, :]
    rows_per_group = Q_HEAD_GROUP * QK_HEAD
    for g0 in range(0, N_HEADS, Q_HEAD_GROUP):
        qt = _dot_nt(w_uqt_ref[g0 * QK_HEAD:g0 * QK_HEAD + rows_per_group, :], cqn)
        for j in range(Q_HEAD_GROUP):
            r = j * QK_HEAD
            o = (g0 + j) * QK_PAD
            x1 = qt[r + QK_NOPE:r + QK_NOPE + half]
            x2 = qt[r + QK_NOPE + half:r + QK_HEAD]
            qt_ref[o:o + QK_NOPE, :] = qt[r:r + QK_NOPE].astype(_bf16)
            qt_ref[o + QK_NOPE:o + QK_NOPE + half, :] = (x1 * cos_h - x2 * sin_h).astype(_bf16)
            qt_ref[o + QK_NOPE + half:o + QK_HEAD, :] = (x2 * cos_h + x1 * sin_h).astype(_bf16)
            qt_ref[o + QK_HEAD:o + QK_PAD, :] = jnp.zeros((QK_PAD - QK_HEAD, tm), _bf16)

    ckv = _dot(h, w_in_ref[:, OFF_CKV:OFF_CKV + KV_LORA])
    ckvn = _rms(ckv, kvg_ref[...]).astype(_bf16)
    kr = _dot(h, w_in_ref[:, OFF_KR:OFF_KR + LANES])
    k_pe = (kr * cos_ref[...] + pltpu.roll(kr, QK_ROPE, axis=1) * sin_ref[...]).astype(_bf16)
    k_nope = _dot(ckvn, w_uk_ref[...]).astype(_bf16)
    for hd in range(N_HEADS):
        k_ref[:, hd * QK_PAD:hd * QK_PAD + QK_NOPE] = k_nope[:, hd * QK_NOPE:(hd + 1) * QK_NOPE]
        k_ref[:, hd * QK_PAD + QK_NOPE:(hd + 1) * QK_PAD] = k_pe
    vt_ref[...] = _dot_nt(w_vt_ref[...], ckvn).astype(_bf16)


def _front(x2, layer, g, w_in, w_pool, ps, qg, kvg, w_uqt, w_uk, w_vt, tables, seq):
    t = x2.shape[0]
    tm = FRONT_TILE
    cos_t, sin_t, cos_tt, sin_tt = tables

    def row(width):
        return pl.BlockSpec((tm, width), lambda i: (i, 0))

    def col(height):
        return pl.BlockSpec((height, tm), lambda i: (0, i))

    def row_out(width):
        return jax.ShapeDtypeStruct((t, width), _bf16)

    def col_out(height):
        return jax.ShapeDtypeStruct((height, t), _bf16)

    weights = (g, w_in, w_pool, ps, qg, kvg, w_uqt, w_uk, w_vt)
    return pl.pallas_call(
        functools.partial(_front_kernel, tm=tm, seq=seq),
        grid=(t // tm,),
        in_specs=[row(D_MODEL)] + [_layer(w, layer) for w in weights]
                 + [row(LANES), row(LANES), col(QK_ROPE), col(QK_ROPE)],
        out_specs=[row(POOL_WIDTH), col(N_HEADS * QK_PAD), row(N_HEADS * QK_PAD), col(N_HEADS * V_HEAD)],
        out_shape=[row_out(POOL_WIDTH), col_out(N_HEADS * QK_PAD), row_out(N_HEADS * QK_PAD),
                   col_out(N_HEADS * V_HEAD)],
        scratch_shapes=[pltpu.VMEM((POOL_HALO, POOL_WIDTH), _f32)],
        compiler_params=_params(("arbitrary",)),
        name="front",
    )(x2, *weights, cos_t, sin_t, cos_tt, sin_tt)


def _attn_kernel(qt_ref, k_ref, vt_ref, o_ref, s0_scr, s1_scr, m_scr, acc_scr, *, tile, block,
                 chunk, per_trip):
    i = pl.program_id(2)
    nb = tile // block

    def scores(q_col0, nq, blk):
        start = pl.multiple_of(blk * block, block)
        return _dot(k_ref[pl.ds(start, block), :], qt_ref[:, pl.ds(q_col0, nq)])

    def values(blk):
        vt = vt_ref[:, pl.ds(pl.multiple_of(blk * block, block), block)]
        return jnp.concatenate([vt, jnp.ones((SUM_ROWS, block), _bf16)], axis=0)

    def absorb(s_ref, s_col0, col0, ncols, vt, key0=None):
        for c in range(ncols // chunk):
            cols = pl.ds(col0 + c * chunk, chunk)
            if key0 is None:
                s = s_ref[:, pl.ds(s_col0 + c * chunk, chunk)]
                vc = vt
            else:
                nkeys = col0 + (c + 1) * chunk - key0
                s = s_ref[pl.ds(0, nkeys), pl.ds(s_col0 + c * chunk, chunk)]
                vc = vt[:, :nkeys]
                key_id = key0 + lax.broadcasted_iota(jnp.int32, s.shape, 0)
                q_id = col0 + c * chunk + lax.broadcasted_iota(jnp.int32, s.shape, 1)
                s = jnp.where(key_id <= q_id, s, MASK_VALUE)
            m_old = m_scr[:, cols]
            m_new = jnp.maximum(m_old, jnp.max(s, axis=0, keepdims=True))
            alpha = jnp.exp2(m_old - m_new)
            p = jnp.exp2(s - m_new)
            m_scr[:, cols] = m_new
            acc_scr[:, cols] = alpha * acc_scr[:, cols] + _dot(vc, p.astype(_bf16))

    def finish(col0, ncols):
        a = acc_scr[:, pl.ds(col0, ncols)]
        o_ref[pl.ds(col0, ncols), :] = (a[:V_HEAD] / a[V_HEAD:V_HEAD + 1]).T.astype(_bf16)

    m_scr[...] = jnp.full(m_scr.shape, MASK_VALUE, _f32)
    acc_scr[...] = jnp.zeros(acc_scr.shape, _f32)
    s0_scr[...] = scores(0, tile, 0)

    bufs = (s0_scr, s1_scr)

    def body(t, carry):
        for kb in range(per_trip):
            blk = per_trip * t + kb
            bufs[(kb + 1) % 2][...] = scores(0, tile, blk + 1)
            absorb(bufs[kb % 2], 0, 0, tile, values(blk))
        return carry

    lax.fori_loop(0, i * (nb // per_trip), body, 0)

    for kb in range(nb):
        cur, nxt = bufs[kb % 2], bufs[(kb + 1) % 2]
        q0 = kb * block
        if kb + 1 < nb:
            nxt[:, pl.ds(0, tile - q0 - block)] = scores(q0 + block, tile - q0 - block, nb * i + kb + 1)
        vt = values(nb * i + kb)
        absorb(cur, 0, q0, block, vt, key0=q0)
        finish(q0, block)
        if kb + 1 < nb:
            absorb(cur, block, q0 + block, tile - q0 - block, vt)


def _attention(qt, k, vt, batch, seq):
    tq = ATTN_TILE
    nq = seq // tq
    stat = pltpu.VMEM((1, tq), _f32)
    score = pltpu.VMEM((ATTN_BLOCK, tq), _f32)
    return pl.pallas_call(
        functools.partial(_attn_kernel, tile=ATTN_TILE, block=ATTN_BLOCK, chunk=ATTN_CHUNK,
                          per_trip=ATTN_BLOCKS_PER_TRIP),
        grid=(batch, N_HEADS, nq),
        in_specs=[pl.BlockSpec((QK_PAD, tq), lambda b, h, i: (h, b * nq + i)),
                  pl.BlockSpec((seq, QK_PAD), lambda b, h, i: (b, h)),
                  pl.BlockSpec((V_HEAD, seq), lambda b, h, i: (h, b))],
        out_specs=pl.BlockSpec((tq, V_HEAD), lambda b, h, i: (b * nq + i, h)),
        out_shape=jax.ShapeDtypeStruct((batch * seq, N_HEADS * V_HEAD), _bf16),
        scratch_shapes=[score, score, stat, pltpu.VMEM((V_HEAD + SUM_ROWS, tq), _f32)],
        compiler_params=_params(("arbitrary", "arbitrary", "arbitrary")),
        name="attention",
    )(qt, k, vt)


def _back_kernel(x_ref, attn_ref, mixed_ref, p_ref, gm_ref, w_gates_ref, w_a_ref, w_b_ref, w_o_ref,
                 gf_ref, w_gate_ref, w_up_ref, w_down_ref, gp_ref, w_pg_ref, w_ple_ref, gl_ref,
                 o_ref, *, final):
    x = x_ref[...]
    hm = _rms(x, gm_ref[...]).astype(_bf16)
    gate_a = jax.nn.sigmoid(_dot(hm, w_gates_ref[:, :D_MODEL]))
    gate_b = jax.nn.sigmoid(_dot(hm, w_gates_ref[:, D_MODEL:]))
    y_a = _dot(mixed_ref[...], w_a_ref[...])
    y_b = _dot(attn_ref[...], w_b_ref[...])
    merged = gate_a * y_a + gate_b * y_b
    x = x + _dot(merged.astype(_bf16), w_o_ref[...])
    h = _rms(x, gf_ref[...]).astype(_bf16)
    acc = None
    for c0 in range(0, D_FF, FF_CHUNK):
        cols = slice(c0, min(c0 + FF_CHUNK, D_FF))
        gate = _dot(h, w_gate_ref[:, cols])
        up = _dot(h, w_up_ref[:, cols])
        act = (gate * jax.nn.sigmoid(gate) * up).astype(_bf16)
        part = _dot(act, w_down_ref[cols, :])
        acc = part if acc is None else acc + part
    x = x + acc
    hp = _rms(x, gp_ref[...]).astype(_bf16)
    gate = jax.nn.sigmoid(_dot(hp, w_pg_ref[...]))
    x = x + gate * _dot(p_ref[...].astype(_bf16), w_ple_ref[...])
    if final:
        x = _rms(x, gl_ref[...])
    o_ref[...] = x


def _back(x2, attn, mixed, p3, layer, weights, gl, final):
    t = x2.shape[0]
    tm = BACK_TILE

    def row(width):
        return pl.BlockSpec((tm, width), lambda i: (i, 0))

    return pl.pallas_call(
        functools.partial(_back_kernel, final=final),
        grid=(t // tm,),
        in_specs=[row(D_MODEL), row(D_MODEL), row(POOL_WIDTH),
                  pl.BlockSpec((None, tm, PLE_DIM), lambda i: (layer, i, 0))]
                 + [_layer(w, layer) for w in weights] + [_resident(gl.shape)],
        out_specs=row(D_MODEL),
        out_shape=jax.ShapeDtypeStruct((t, D_MODEL), _f32),
        compiler_params=_params(("arbitrary",)),
        name="back",
    )(x2, attn, mixed, p3, *weights, gl)


def _w_in_layout_kernel(wt_ref, front_ref, gates_ref):
    head = wt_ref[:OFF_KR, :].T
    b = wt_ref[OFF_KR:OFF_KR + LANES, :].T
    tail = wt_ref[OFF_KR + QK_ROPE:, :].T
    lane = lax.broadcasted_iota(jnp.int32, b.shape, 1)
    half = QK_ROPE // 2
    kr = jnp.where(lane < QK_ROPE, b,
                   jnp.where(lane < QK_ROPE + half, -pltpu.roll(b, half, axis=1),
                             pltpu.roll(b, QK_ROPE + half, axis=1)))
    front_ref[...] = jnp.concatenate([head, kr], axis=1).astype(_bf16)
    gates_ref[...] = tail.astype(_bf16)


def _prep_w_in(w_in):
    depth, rows, width = w_in.shape
    tr = W_IN_LAYOUT_ROWS
    return pl.pallas_call(
        _w_in_layout_kernel,
        grid=(depth, rows // tr),
        in_specs=[pl.BlockSpec((None, width, tr), lambda l, r: (l, 0, r))],
        out_specs=[pl.BlockSpec((None, tr, FRONT_WIDTH), lambda l, r: (l, r, 0)),
                   pl.BlockSpec((None, tr, GATE_WIDTH), lambda l, r: (l, r, 0))],
        out_shape=[jax.ShapeDtypeStruct((depth, rows, FRONT_WIDTH), _bf16),
                   jax.ShapeDtypeStruct((depth, rows, GATE_WIDTH), _bf16)],
        compiler_params=_params(("arbitrary", "arbitrary")),
        name="w_in_layout",
    )(jnp.swapaxes(w_in, 1, 2))


def _prep_w_uq_t(w_uq):
    scale = QK_HEAD ** -0.5 * LOG2E
    w = (w_uq * scale).astype(_bf16)
    return jnp.swapaxes(w.reshape(w.shape[0], Q_LORA, N_HEADS * QK_HEAD), 1, 2)


def kernel(x, p, positions, norm_mix, w_in, w_pool, pool_scale, q_norm, kv_norm, w_uq, w_ukv, w_a, w_b, w_o,
           norm_ffn, w_gate, w_up, w_down, norm_ple, w_ple_gate, w_ple, final_norm):
    batch, seq, _ = x.shape
    depth = w_in.shape[0]
    t = batch * seq
    assert seq % FRONT_TILE == 0 and t % BACK_TILE == 0 and seq % ATTN_TILE == 0
    assert ATTN_TILE % (ATTN_BLOCKS_PER_TRIP * ATTN_BLOCK) == 0 and ATTN_BLOCK % ATTN_CHUNK == 0
    assert t % ROPE_TILE == 0 and N_HEADS % Q_HEAD_GROUP == 0

    inv_freq = 1.0 / (ROPE_THETA ** (jnp.arange(0, QK_ROPE, 2, dtype=_f32) / QK_ROPE))
    tables = _rope_tables(positions.astype(_f32).reshape(1, t), inv_freq[:, None])

    def vec(a):
        return a[:, None, :]

    def cast(a):
        return a.astype(_bf16)

    g_mix, g_q, g_kv, g_ffn, g_ple = vec(norm_mix), vec(q_norm), vec(kv_norm), vec(norm_ffn), vec(norm_ple)
    ps = vec(pool_scale)
    w_front, w_gates = _prep_w_in(w_in)
    w_pool_b = cast(w_pool)
    w_uqt = _prep_w_uq_t(w_uq)
    w_ukv_b = cast(w_ukv)
    w_uk = w_ukv_b[..., :QK_NOPE].reshape(depth, KV_LORA, N_HEADS * QK_NOPE)
    w_vt = jnp.swapaxes(w_ukv_b[..., QK_NOPE:].reshape(depth, KV_LORA, N_HEADS * V_HEAD), 1, 2)
    w_a_b, w_b_b, w_o_b = cast(w_a), cast(w_b), cast(w_o)
    w_gate_b, w_up_b, w_down_b = cast(w_gate), cast(w_up), cast(w_down)
    w_pg_b, w_ple_b = cast(w_ple_gate), cast(w_ple)
    gl = final_norm.reshape(1, D_MODEL)
    p3 = p.reshape(depth, t, PLE_DIM)

    x2 = x.reshape(t, D_MODEL)
    for i in range(depth):
        mixed, qt, k, vt = _front(x2, i, g_mix, w_front, w_pool_b, ps, g_q, g_kv, w_uqt, w_uk, w_vt, tables, seq)
        attn = _attention(qt, k, vt, batch, seq)
        back_weights = (g_mix, w_gates, w_a_b, w_b_b, w_o_b, g_ffn, w_gate_b, w_up_b, w_down_b, g_ple, w_pg_b,
                        w_ple_b)
        x2 = _back(x2, attn, mixed, p3, i, back_weights, gl, final=(i == depth - 1))
    return x2.reshape(batch, seq, D_MODEL)
```

```python
import functools

import jax
import jax.numpy as jnp
from jax import lax
from jax.experimental import pallas as pl
from jax.experimental.pallas import tpu as pltpu

D_MODEL = 1024
PLE_DIM = 256
POOL_WINDOWS = (2, 4, 8, 16)
POOL_GROUP = 128
POOL_WIDTH = 512
N_HEADS = 8
Q_LORA = 512
KV_LORA = 256
QK_NOPE = 128
QK_ROPE = 64
QK_HEAD = QK_NOPE + QK_ROPE
V_HEAD = 128
D_FF = 2816
ROPE_THETA = 10000.0
EPS = 1e-6

LANES = 128
QK_PAD = 2 * LANES
POOL_HALO = 16

OFF_U = 0
OFF_CQ = OFF_U + POOL_WIDTH
OFF_CKV = OFF_CQ + Q_LORA
OFF_KR = OFF_CKV + KV_LORA
FRONT_WIDTH = OFF_KR + LANES
GATE_WIDTH = 2 * D_MODEL

FRONT_TILE = 1024
BACK_TILE = 512
ROPE_TILE = 2048
W_IN_LAYOUT_ROWS = 256
Q_HEAD_GROUP = 4
ATTN_TILE = 4096
ATTN_BLOCK = 512
ATTN_BLOCKS_PER_TRIP = 4
ATTN_CHUNK = 256
SUM_ROWS = 16
FF_CHUNK = 256
VMEM_LIMIT = 60 * 1024 * 1024

MASK_VALUE = -1e30
LOG2E = 1.4426950408889634

_f32 = jnp.float32
_bf16 = jnp.bfloat16
_NT = (((1,), (1,)), ((), ()))


def _dot(a, b):
    return jnp.dot(a, b, preferred_element_type=_f32)


def _dot_nt(a, b):
    return lax.dot_general(a, b, _NT, preferred_element_type=_f32)


def _rms(x, g):
    return x * lax.rsqrt(jnp.mean(x * x, axis=-1, keepdims=True) + EPS) * g


def _resident(shape):
    nd = len(shape)
    return pl.BlockSpec(shape, lambda *_: (0,) * nd, pipeline_mode=pl.Buffered(1))


def _layer(arr, layer):
    nd = arr.ndim - 1
    return pl.BlockSpec((None,) + arr.shape[1:], lambda *_: (layer,) + (0,) * nd, pipeline_mode=pl.Buffered(1))


def _params(sem):
    return pltpu.CompilerParams(dimension_semantics=sem, vmem_limit_bytes=VMEM_LIMIT)


def _rope_table_kernel(pos_ref, inv_ref, cos_ref, sin_ref, cost_ref, sint_ref):
    ang = inv_ref[...] * pos_ref[...]
    c = jnp.cos(ang)
    s = jnp.sin(ang)
    zeros = jnp.zeros((LANES - QK_ROPE, ang.shape[1]), _f32)
    ct = jnp.concatenate([c, c], axis=0)
    st = jnp.concatenate([s, s], axis=0)
    cost_ref[...] = ct
    sint_ref[...] = st
    cos_ref[...] = jnp.concatenate([ct, zeros], axis=0).T
    sin_ref[...] = jnp.concatenate([st, zeros], axis=0).T


def _rope_tables(pos_row, inv_col):
    t = pos_row.shape[1]
    tm = ROPE_TILE
    row = pl.BlockSpec((tm, LANES), lambda i: (i, 0))
    col = pl.BlockSpec((QK_ROPE, tm), lambda i: (0, i))
    return pl.pallas_call(
        _rope_table_kernel,
        grid=(t // tm,),
        in_specs=[pl.BlockSpec((1, tm), lambda i: (0, i)), _resident(inv_col.shape)],
        out_specs=[row, row, col, col],
        out_shape=[jax.ShapeDtypeStruct((t, LANES), _f32)] * 2 + [jax.ShapeDtypeStruct((QK_ROPE, t), _f32)] * 2,
        compiler_params=_params(("arbitrary",)),
        name="rope_tables",
    )(pos_row, inv_col)


def _front_kernel(x_ref, g_ref, w_in_ref, w_pool_ref, ps_ref, qg_ref, kvg_ref, w_uqt_ref, w_uk_ref, w_vt_ref,
                  cos_ref, sin_ref, cost_ref, sint_ref,
                  mixed_ref, qt_ref, k_ref, vt_ref,
                  carry_ref, *, tm, seq):
    i = pl.program_id(0)
    t0 = (i * tm) % seq

    @pl.when(t0 == 0)
    def _():
        carry_ref[...] = jnp.zeros_like(carry_ref)

    h = _rms(x_ref[...], g_ref[...]).astype(_bf16)

    u = _dot(h, w_in_ref[:, OFF_U:OFF_U + POOL_WIDTH])
    ext = jnp.concatenate([carry_ref[...], u], axis=0)
    carry_ref[...] = u[tm - POOL_HALO:, :]
    sums = []
    cur = ext
    step = 1
    for _ in POOL_WINDOWS:
        cur = cur + pltpu.roll(cur, step, axis=0)
        sums.append(cur[POOL_HALO:, :POOL_GROUP])
        cur = cur[:, POOL_GROUP:]
        step *= 2
    t_idx = t0 + lax.broadcasted_iota(jnp.int32, (tm, POOL_GROUP), 0)
    for g, w in enumerate(POOL_WINDOWS):
        cols = slice(g * POOL_GROUP, (g + 1) * POOL_GROUP)
        cnt = jnp.minimum(t_idx + 1, w).astype(_f32)
        pooled = sums[g] / cnt - u[:, cols]
        mixed = _dot(pooled.astype(_bf16), w_pool_ref[g]) * ps_ref[:, cols]
        mixed_ref[:, cols] = mixed.astype(_bf16)

    cq = _dot(h, w_in_ref[:, OFF_CQ:OFF_CQ + Q_LORA])
    cqn = _rms(cq, qg_ref[...]).astype(_bf16)
    half = QK_ROPE // 2
    cos_h = cost_ref[:half, :]
    sin_h = sint_ref[:half, :]
    rows_per_group = Q_HEAD_GROUP * QK_HEAD
    for g0 in range(0, N_HEADS, Q_HEAD_GROUP):
        qt = _dot_nt(w_uqt_ref[g0 * QK_HEAD:g0 * QK_HEAD + rows_per_group, :], cqn)
        for j in range(Q_HEAD_GROUP):
            r = j * QK_HEAD
            o = (g0 + j) * QK_PAD
            x1 = qt[r + QK_NOPE:r + QK_NOPE + half]
            x2 = qt[r + QK_NOPE + half:r + QK_HEAD]
            qt_ref[o:o + QK_NOPE, :] = qt[r:r + QK_NOPE].astype(_bf16)
            qt_ref[o + QK_NOPE:o + QK_NOPE + half, :] = (x1 * cos_h - x2 * sin_h).astype(_bf16)
            qt_ref[o + QK_NOPE + half:o + QK_HEAD, :] = (x2 * cos_h + x1 * sin_h).astype(_bf16)
            qt_ref[o + QK_HEAD:o + QK_PAD, :] = jnp.zeros((QK_PAD - QK_HEAD, tm), _bf16)

    ckv = _dot(h, w_in_ref[:, OFF_CKV:OFF_CKV + KV_LORA])
    ckvn = _rms(ckv, kvg_ref[...]).astype(_bf16)
    kr = _dot(h, w_in_ref[:, OFF_KR:OFF_KR + LANES])
    k_pe = (kr * cos_ref[...] + pltpu.roll(kr, QK_ROPE, axis=1) * sin_ref[...]).astype(_bf16)
    k_nope = _dot(ckvn, w_uk_ref[...]).astype(_bf16)
    for hd in range(N_HEADS):
        k_ref[:, hd * QK_PAD:hd * QK_PAD + QK_NOPE] = k_nope[:, hd * QK_NOPE:(hd + 1) * QK_NOPE]
        k_ref[:, hd * QK_PAD + QK_NOPE:(hd + 1) * QK_PAD] = k_pe
    vt_ref[...] = _dot_nt(w_vt_ref[...], ckvn).astype(_bf16)


def _front(x2, layer, g, w_in, w_pool, ps, qg, kvg, w_uqt, w_uk, w_vt, tables, seq):
    t = x2.shape[0]
    tm = FRONT_TILE
    cos_t, sin_t, cos_tt, sin_tt = tables

    def row(width):
        return pl.BlockSpec((tm, width), lambda i: (i, 0))

    def col(height):
        return pl.BlockSpec((height, tm), lambda i: (0, i))

    def row_out(width):
        return jax.ShapeDtypeStruct((t, width), _bf16)

    def col_out(height):
        return jax.ShapeDtypeStruct((height, t), _bf16)

    weights = (g, w_in, w_pool, ps, qg, kvg, w_uqt, w_uk, w_vt)
    return pl.pallas_call(
        functools.partial(_front_kernel, tm=tm, seq=seq),
        grid=(t // tm,),
        in_specs=[row(D_MODEL)] + [_layer(w, layer) for w in weights]
                 + [row(LANES), row(LANES), col(QK_ROPE), col(QK_ROPE)],
        out_specs=[row(POOL_WIDTH), col(N_HEADS * QK_PAD), row(N_HEADS * QK_PAD), col(N_HEADS * V_HEAD)],
        out_shape=[row_out(POOL_WIDTH), col_out(N_HEADS * QK_PAD), row_out(N_HEADS * QK_PAD),
                   col_out(N_HEADS * V_HEAD)],
        scratch_shapes=[pltpu.VMEM((POOL_HALO, POOL_WIDTH), _f32)],
        compiler_params=_params(("arbitrary",)),
        name="front",
    )(x2, *weights, cos_t, sin_t, cos_tt, sin_tt)


def _attn_kernel(qt_ref, k_ref, vt_ref, o_ref, s0_scr, s1_scr, m_scr, acc_scr, *, tile, block,
                 chunk, per_trip):
    i = pl.program_id(2)
    nb = tile // block

    def scores(q_col0, nq, blk):
        start = pl.multiple_of(blk * block, block)
        return _dot(k_ref[pl.ds(start, block), :], qt_ref[:, pl.ds(q_col0, nq)])

    def values(blk):
        vt = vt_ref[:, pl.ds(pl.multiple_of(blk * block, block), block)]
        return jnp.concatenate([vt, jnp.ones((SUM_ROWS, block), _bf16)], axis=0)

    def absorb(s_ref, s_col0, col0, ncols, vt, key0=None):
        for c in range(ncols // chunk):
            cols = pl.ds(col0 + c * chunk, chunk)
            if key0 is None:
                s = s_ref[:, pl.ds(s_col0 + c * chunk, chunk)]
                vc = vt
            else:
                nkeys = col0 + (c + 1) * chunk - key0
                s = s_ref[pl.ds(0, nkeys), pl.ds(s_col0 + c * chunk, chunk)]
                vc = vt[:, :nkeys]
                key_id = key0 + lax.broadcasted_iota(jnp.int32, s.shape, 0)
                q_id = col0 + c * chunk + lax.broadcasted_iota(jnp.int32, s.shape, 1)
                s = jnp.where(key_id <= q_id, s, MASK_VALUE)
            m_old = m_scr[:, cols]
            m_new = jnp.maximum(m_old, jnp.max(s, axis=0, keepdims=True))
            alpha = jnp.exp2(m_old - m_new)
            p = jnp.exp2(s - m_new)
            m_scr[:, cols] = m_new
            acc_scr[:, cols] = alpha * acc_scr[:, cols] + _dot(vc, p.astype(_bf16))

    def finish(col0, ncols):
        a = acc_scr[:, pl.ds(col0, ncols)]
        o_ref[pl.ds(col0, ncols), :] = (a[:V_HEAD] / a[V_HEAD:V_HEAD + 1]).T.astype(_bf16)

    m_scr[...] = jnp.full(m_scr.shape, MASK_VALUE, _f32)
    acc_scr[...] = jnp.zeros(acc_scr.shape, _f32)
    s0_scr[...] = scores(0, tile, 0)

    bufs = (s0_scr, s1_scr)

    def body(t, carry):
        for kb in range(per_trip):
            blk = per_trip * t + kb
            bufs[(kb + 1) % 2][...] = scores(0, tile, blk + 1)
            absorb(bufs[kb % 2], 0, 0, tile, values(blk))
        return carry

    lax.fori_loop(0, i * (nb // per_trip), body, 0)

    for kb in range(nb):
        cur, nxt = bufs[kb % 2], bufs[(kb + 1) % 2]
        q0 = kb * block
        if kb + 1 < nb:
            nxt[:, pl.ds(0, tile - q0 - block)] = scores(q0 + block, tile - q0 - block, nb * i + kb + 1)
        vt = values(nb * i + kb)
        absorb(cur, 0, q0, block, vt, key0=q0)
        finish(q0, block)
        if kb + 1 < nb:
            absorb(cur, block, q0 + block, tile - q0 - block, vt)


def _attention(qt, k, vt, batch, seq):
    tq = ATTN_TILE
    nq = seq // tq
    stat = pltpu.VMEM((1, tq), _f32)
    score = pltpu.VMEM((ATTN_BLOCK, tq), _f32)
    return pl.pallas_call(
        functools.partial(_attn_kernel, tile=ATTN_TILE, block=ATTN_BLOCK, chunk=ATTN_CHUNK,
                          per_trip=ATTN_BLOCKS_PER_TRIP),
        grid=(batch, N_HEADS, nq),
        in_specs=[pl.BlockSpec((QK_PAD, tq), lambda b, h, i: (h, b * nq + i)),
                  pl.BlockSpec((seq, QK_PAD), lambda b, h, i: (b, h)),
                  pl.BlockSpec((V_HEAD, seq), lambda b, h, i: (h, b))],
        out_specs=pl.BlockSpec((tq, V_HEAD), lambda b, h, i: (b * nq + i, h)),
        out_shape=jax.ShapeDtypeStruct((batch * seq, N_HEADS * V_HEAD), _bf16),
        scratch_shapes=[score, score, stat, pltpu.VMEM((V_HEAD + SUM_ROWS, tq), _f32)],
        compiler_params=_params(("arbitrary", "arbitrary", "arbitrary")),
        name="attention",
    )(qt, k, vt)


def _back_kernel(x_ref, attn_ref, mixed_ref, p_ref, gm_ref, w_gates_ref, w_a_ref, w_b_ref, w_o_ref,
                 gf_ref, w_gate_ref, w_up_ref, w_down_ref, gp_ref, w_pg_ref, w_ple_ref, gl_ref,
                 o_ref, *, final):
    x = x_ref[...]
    hm = _rms(x, gm_ref[...]).astype(_bf16)
    gate_a = jax.nn.sigmoid(_dot(hm, w_gates_ref[:, :D_MODEL]))
    gate_b = jax.nn.sigmoid(_dot(hm, w_gates_ref[:, D_MODEL:]))
    y_a = _dot(mixed_ref[...], w_a_ref[...])
    y_b = _dot(attn_ref[...], w_b_ref[...])
    merged = gate_a * y_a + gate_b * y_b
    x = x + _dot(merged.astype(_bf16), w_o_ref[...])
    h = _rms(x, gf_ref[...]).astype(_bf16)
    acc = None
    for c0 in range(0, D_FF, FF_CHUNK):
        cols = slice(c0, min(c0 + FF_CHUNK, D_FF))
        gate = _dot(h, w_gate_ref[:, cols])
        up = _dot(h, w_up_ref[:, cols])
        act = (gate * jax.nn.sigmoid(gate) * up).astype(_bf16)
        part = _dot(act, w_down_ref[cols, :])
        acc = part if acc is None else acc + part
    x = x + acc
    hp = _rms(x, gp_ref[...]).astype(_bf16)
    gate = jax.nn.sigmoid(_dot(hp, w_pg_ref[...]))
    x = x + gate * _dot(p_ref[...].astype(_bf16), w_ple_ref[...])
    if final:
        x = _rms(x, gl_ref[...])
    o_ref[...] = x


def _back(x2, attn, mixed, p3, layer, weights, gl, final):
    t = x2.shape[0]
    tm = BACK_TILE

    def row(width):
        return pl.BlockSpec((tm, width), lambda i: (i, 0))

    return pl.pallas_call(
        functools.partial(_back_kernel, final=final),
        grid=(t // tm,),
        in_specs=[row(D_MODEL), row(D_MODEL), row(POOL_WIDTH),
                  pl.BlockSpec((None, tm, PLE_DIM), lambda i: (layer, i, 0))]
                 + [_layer(w, layer) for w in weights] + [_resident(gl.shape)],
        out_specs=row(D_MODEL),
        out_shape=jax.ShapeDtypeStruct((t, D_MODEL), _f32),
        compiler_params=_params(("arbitrary",)),
        name="back",
    )(x2, attn, mixed, p3, *weights, gl)


def _w_in_layout_kernel(wt_ref, front_ref, gates_ref):
    head = wt_ref[:OFF_KR, :].T
    b = wt_ref[OFF_KR:OFF_KR + LANES, :].T
    tail = wt_ref[OFF_KR + QK_ROPE:, :].T
    lane = lax.broadcasted_iota(jnp.int32, b.shape, 1)
    half = QK_ROPE // 2
    kr = jnp.where(lane < QK_ROPE, b,
                   jnp.where(lane < QK_ROPE + half, -pltpu.roll(b, half, axis=1),
                             pltpu.roll(b, QK_ROPE + half, axis=1)))
    front_ref[...] = jnp.concatenate([head, kr], axis=1).astype(_bf16)
    gates_ref[...] = tail.astype(_bf16)


def _prep_w_in(w_in):
    depth, rows, width = w_in.shape
    tr = W_IN_LAYOUT_ROWS
    return pl.pallas_call(
        _w_in_layout_kernel,
        grid=(depth, rows // tr),
        in_specs=[pl.BlockSpec((None, width, tr), lambda l, r: (l, 0, r))],
        out_specs=[pl.BlockSpec((None, tr, FRONT_WIDTH), lambda l, r: (l, r, 0)),
                   pl.BlockSpec((None, tr, GATE_WIDTH), lambda l, r: (l, r, 0))],
        out_shape=[jax.ShapeDtypeStruct((depth, rows, FRONT_WIDTH), _bf16),
                   jax.ShapeDtypeStruct((depth, rows, GATE_WIDTH), _bf16)],
        compiler_params=_params(("arbitrary", "arbitrary")),
        name="w_in_layout",
    )(jnp.swapaxes(w_in, 1, 2))


def _prep_w_uq_t(w_uq):
    scale = QK_HEAD ** -0.5 * LOG2E
    w = (w_uq * scale).astype(_bf16)
    return jnp.swapaxes(w.reshape(w.shape[0], Q_LORA, N_HEADS * QK_HEAD), 1, 2)


def kernel(x, p, positions, norm_mix, w_in, w_pool, pool_scale, q_norm, kv_norm, w_uq, w_ukv, w_a, w_b, w_o,
           norm_ffn, w_gate, w_up, w_down, norm_ple, w_ple_gate, w_ple, final_norm):
    batch, seq, _ = x.shape
    depth = w_in.shape[0]
    t = batch * seq
    assert seq % FRONT_TILE == 0 and t % BACK_TILE == 0 and seq % ATTN_TILE == 0
    assert ATTN_TILE % (ATTN_BLOCKS_PER_TRIP * ATTN_BLOCK) == 0 and ATTN_BLOCK % ATTN_CHUNK == 0
    assert t % ROPE_TILE == 0 and N_HEADS % Q_HEAD_GROUP == 0

    inv_freq = 1.0 / (ROPE_THETA ** (jnp.arange(0, QK_ROPE, 2, dtype=_f32) / QK_ROPE))
    tables = _rope_tables(positions.astype(_f32).reshape(1, t), inv_freq[:, None])

    def vec(a):
        return a[:, None, :]

    def cast(a):
        return a.astype(_bf16)

    g_mix, g_q, g_kv, g_ffn, g_ple = vec(norm_mix), vec(q_norm), vec(kv_norm), vec(norm_ffn), vec(norm_ple)
    ps = vec(pool_scale)
    w_front, w_gates = _prep_w_in(w_in)
    w_pool_b = cast(w_pool)
    w_uqt = _prep_w_uq_t(w_uq)
    w_ukv_b = cast(w_ukv)
    w_uk = w_ukv_b[..., :QK_NOPE].reshape(depth, KV_LORA, N_HEADS * QK_NOPE)
    w_vt = jnp.swapaxes(w_ukv_b[..., QK_NOPE:].reshape(depth, KV_LORA, N_HEADS * V_HEAD), 1, 2)
    w_a_b, w_b_b, w_o_b = cast(w_a), cast(w_b), cast(w_o)
    w_gate_b, w_up_b, w_down_b = cast(w_gate), cast(w_up), cast(w_down)
    w_pg_b, w_ple_b = cast(w_ple_gate), cast(w_ple)
    gl = final_norm.reshape(1, D_MODEL)
    p3 = p.reshape(depth, t, PLE_DIM)

    x2 = x.reshape(t, D_MODEL)
    for i in range(depth):
        mixed, qt, k, vt = _front(x2, i, g_mix, w_front, w_pool_b, ps, g_q, g_kv, w_uqt, w_uk, w_vt, tables, seq)
        attn = _attention(qt, k, vt, batch, seq)
        back_weights = (g_mix, w_gates, w_a_b, w_b_b, w_o_b, g_ffn, w_gate_b, w_up_b, w_down_b, g_ple, w_pg_b,
                        w_ple_b)
        x2 = _back(x2, attn, mixed, p3, i, back_weights, gl, final=(i == depth - 1))
    return x2.reshape(batch, seq, D_MODEL)
```

```python
import functools

import jax
import jax.numpy as jnp
from jax import lax
from jax.experimental import pallas as pl
from jax.experimental.pallas import tpu as pltpu

D_MODEL = 1024
PLE_DIM = 256
POOL_WINDOWS = (2, 4, 8, 16)
POOL_GROUP = 128
POOL_WIDTH = 512
N_HEADS = 8
Q_LORA = 512
KV_LORA = 256
QK_NOPE = 128
QK_ROPE = 64
QK_HEAD = QK_NOPE + QK_ROPE
V_HEAD = 128
D_FF = 2816
ROPE_THETA = 10000.0
EPS = 1e-6

LANES = 128
QK_PAD = 2 * LANES
POOL_HALO = 16

OFF_U = 0
OFF_CQ = OFF_U + POOL_WIDTH
OFF_CKV = OFF_CQ + Q_LORA
OFF_KR = OFF_CKV + KV_LORA
FRONT_WIDTH = OFF_KR + LANES
GATE_WIDTH = 2 * D_MODEL

FRONT_TILE = 1024
BACK_TILE = 512
ROPE_TILE = 2048
W_IN_LAYOUT_ROWS = 256
Q_HEAD_GROUP = 4
ATTN_TILE = 4096
ATTN_BLOCK = 512
ATTN_BLOCKS_PER_TRIP = 4
ATTN_CHUNK = 256
SUM_ROWS = 16
FF_CHUNK = 256
VMEM_LIMIT = 60 * 1024 * 1024

MASK_VALUE = -1e30
LOG2E = 1.4426950408889634

_f32 = jnp.float32
_bf16 = jnp.bfloat16
_NT = (((1,), (1,)), ((), ()))


def _dot(a, b):
    return jnp.dot(a, b, preferred_element_type=_f32)


def _dot_nt(a, b):
    return lax.dot_general(a, b, _NT, preferred_element_type=_f32)


def _rms(x, g):
    return x * lax.rsqrt(jnp.mean(x * x, axis=-1, keepdims=True) + EPS) * g


def _resident(shape):
    nd = len(shape)
    return pl.BlockSpec(shape, lambda *_: (0,) * nd, pipeline_mode=pl.Buffered(1))


def _layer(arr, layer):
    nd = arr.ndim - 1
    return pl.BlockSpec((None,) + arr.shape[1:], lambda *_: (layer,) + (0,) * nd, pipeline_mode=pl.Buffered(1))


def _params(sem):
    return pltpu.CompilerParams(dimension_semantics=sem, vmem_limit_bytes=VMEM_LIMIT)


def _rope_table_kernel(pos_ref, inv_ref, cos_ref, sin_ref, cost_ref, sint_ref):
    ang = inv_ref[...] * pos_ref[...]
    c = jnp.cos(ang)
    s = jnp.sin(ang)
    zeros = jnp.zeros((LANES - QK_ROPE, ang.shape[1]), _f32)
    ct = jnp.concatenate([c, c], axis=0)
    st = jnp.concatenate([s, s], axis=0)
    cost_ref[...] = ct
    sint_ref[...] = st
    cos_ref[...] = jnp.concatenate([ct, zeros], axis=0).T
    sin_ref[...] = jnp.concatenate([st, zeros], axis=0).T


def _rope_tables(pos_row, inv_col):
    t = pos_row.shape[1]
    tm = ROPE_TILE
    row = pl.BlockSpec((tm, LANES), lambda i: (i, 0))
    col = pl.BlockSpec((QK_ROPE, tm), lambda i: (0, i))
    return pl.pallas_call(
        _rope_table_kernel,
        grid=(t // tm,),
        in_specs=[pl.BlockSpec((1, tm), lambda i: (0, i)), _resident(inv_col.shape)],
        out_specs=[row, row, col, col],
        out_shape=[jax.ShapeDtypeStruct((t, LANES), _f32)] * 2 + [jax.ShapeDtypeStruct((QK_ROPE, t), _f32)] * 2,
        compiler_params=_params(("arbitrary",)),
        name="rope_tables",
    )(pos_row, inv_col)


def _pool_mixer(u, carry_ref, w_pool_ref, ps_ref, t0):
    tm = u.shape[0]
    ext = jnp.concatenate([carry_ref[...], u], axis=0)
    carry_ref[...] = u[tm - POOL_HALO:, :]
    sums = []
    cur = ext
    step = 1
    for _ in POOL_WINDOWS:
        cur = cur + pltpu.roll(cur, step, axis=0)
        sums.append(cur[POOL_HALO:, :POOL_GROUP])
        cur = cur[:, POOL_GROUP:]
        step *= 2
    t_idx = t0 + lax.broadcasted_iota(jnp.int32, (tm, POOL_GROUP), 0)
    mixed = []
    for g, w in enumerate(POOL_WINDOWS):
        cols = slice(g * POOL_GROUP, (g + 1) * POOL_GROUP)
        cnt = jnp.minimum(t_idx + 1, w).astype(_f32)
        pooled = sums[g] / cnt - u[:, cols]
        mixed.append((_dot(pooled.astype(_bf16), w_pool_ref[g]) * ps_ref[:, cols]).astype(_bf16))
    return jnp.concatenate(mixed, axis=1)


def _front_kernel(x_ref, g_ref, w_in_ref, qg_ref, kvg_ref, w_uqt_ref, w_uk_ref, w_vt_ref,
                  cos_ref, sin_ref, cost_ref, sint_ref,
                  qt_ref, k_ref, vt_ref, *, tm):
    h = _rms(x_ref[...], g_ref[...]).astype(_bf16)

    cq = _dot(h, w_in_ref[:, OFF_CQ:OFF_CQ + Q_LORA])
    cqn = _rms(cq, qg_ref[...]).astype(_bf16)
    half = QK_ROPE // 2
    cos_h = cost_ref[:half, :]
    sin_h = sint_ref[:half, :]
    rows_per_group = Q_HEAD_GROUP * QK_HEAD
    for g0 in range(0, N_HEADS, Q_HEAD_GROUP):
        qt = _dot_nt(w_uqt_ref[g0 * QK_HEAD:g0 * QK_HEAD + rows_per_group, :], cqn)
        for j in range(Q_HEAD_GROUP):
            r = j * QK_HEAD
            o = (g0 + j) * QK_PAD
            x1 = qt[r + QK_NOPE:r + QK_NOPE + half]
            x2 = qt[r + QK_NOPE + half:r + QK_HEAD]
            qt_ref[o:o + QK_NOPE, :] = qt[r:r + QK_NOPE].astype(_bf16)
            qt_ref[o + QK_NOPE:o + QK_NOPE + half, :] = (x1 * cos_h - x2 * sin_h).astype(_bf16)
            qt_ref[o + QK_NOPE + half:o + QK_HEAD, :] = (x2 * cos_h + x1 * sin_h).astype(_bf16)
            qt_ref[o + QK_HEAD:o + QK_PAD, :] = jnp.zeros((QK_PAD - QK_HEAD, tm), _bf16)

    ckv = _dot(h, w_in_ref[:, OFF_CKV:OFF_CKV + KV_LORA])
    ckvn = _rms(ckv, kvg_ref[...]).astype(_bf16)
    kr = _dot(h, w_in_ref[:, OFF_KR:OFF_KR + LANES])
    k_pe = (kr * cos_ref[...] + pltpu.roll(kr, QK_ROPE, axis=1) * sin_ref[...]).astype(_bf16)
    k_nope = _dot(ckvn, w_uk_ref[...]).astype(_bf16)
    for hd in range(N_HEADS):
        k_ref[:, hd * QK_PAD:hd * QK_PAD + QK_NOPE] = k_nope[:, hd * QK_NOPE:(hd + 1) * QK_NOPE]
        k_ref[:, hd * QK_PAD + QK_NOPE:(hd + 1) * QK_PAD] = k_pe
    vt_ref[...] = _dot_nt(w_vt_ref[...], ckvn).astype(_bf16)


def _front(x2, layer, g, w_in, qg, kvg, w_uqt, w_uk, w_vt, tables):
    t = x2.shape[0]
    tm = FRONT_TILE
    cos_t, sin_t, cos_tt, sin_tt = tables

    def row(width):
        return pl.BlockSpec((tm, width), lambda i: (i, 0))

    def col(height):
        return pl.BlockSpec((height, tm), lambda i: (0, i))

    def row_out(width):
        return jax.ShapeDtypeStruct((t, width), _bf16)

    def col_out(height):
        return jax.ShapeDtypeStruct((height, t), _bf16)

    weights = (g, w_in, qg, kvg, w_uqt, w_uk, w_vt)
    return pl.pallas_call(
        functools.partial(_front_kernel, tm=tm),
        grid=(t // tm,),
        in_specs=[row(D_MODEL)] + [_layer(w, layer) for w in weights]
                 + [row(LANES), row(LANES), col(QK_ROPE), col(QK_ROPE)],
        out_specs=[col(N_HEADS * QK_PAD), row(N_HEADS * QK_PAD), col(N_HEADS * V_HEAD)],
        out_shape=[col_out(N_HEADS * QK_PAD), row_out(N_HEADS * QK_PAD), col_out(N_HEADS * V_HEAD)],
        compiler_params=_params(("arbitrary",)),
        name="front",
    )(x2, *weights, cos_t, sin_t, cos_tt, sin_tt)


def _attn_kernel(qt_ref, k_ref, vt_ref, o_ref, s0_scr, s1_scr, m_scr, acc_scr, *, tile, block,
                 chunk, per_trip):
    i = pl.program_id(2)
    nb = tile // block

    def scores(q_col0, nq, blk):
        start = pl.multiple_of(blk * block, block)
        return _dot(k_ref[pl.ds(start, block), :], qt_ref[:, pl.ds(q_col0, nq)])

    def values(blk):
        vt = vt_ref[:, pl.ds(pl.multiple_of(blk * block, block), block)]
        return jnp.concatenate([vt, jnp.ones((SUM_ROWS, block), _bf16)], axis=0)

    def absorb(s_ref, s_col0, col0, ncols, vt, key0=None):
        for c in range(ncols // chunk):
            cols = pl.ds(col0 + c * chunk, chunk)
            if key0 is None:
                s = s_ref[:, pl.ds(s_col0 + c * chunk, chunk)]
                vc = vt
            else:
                nkeys = col0 + (c + 1) * chunk - key0
                s = s_ref[pl.ds(0, nkeys), pl.ds(s_col0 + c * chunk, chunk)]
                vc = vt[:, :nkeys]
                key_id = key0 + lax.broadcasted_iota(jnp.int32, s.shape, 0)
                q_id = col0 + c * chunk + lax.broadcasted_iota(jnp.int32, s.shape, 1)
                s = jnp.where(key_id <= q_id, s, MASK_VALUE)
            m_old = m_scr[:, cols]
            m_new = jnp.maximum(m_old, jnp.max(s, axis=0, keepdims=True))
            alpha = jnp.exp2(m_old - m_new)
            p = jnp.exp2(s - m_new)
            m_scr[:, cols] = m_new
            acc_scr[:, cols] = alpha * acc_scr[:, cols] + _dot(vc, p.astype(_bf16))

    def finish(col0, ncols):
        a = acc_scr[:, pl.ds(col0, ncols)]
        o_ref[pl.ds(col0, ncols), :] = (a[:V_HEAD] / a[V_HEAD:V_HEAD + 1]).T.astype(_bf16)

    m_scr[...] = jnp.full(m_scr.shape, MASK_VALUE, _f32)
    acc_scr[...] = jnp.zeros(acc_scr.shape, _f32)
    s0_scr[...] = scores(0, tile, 0)

    bufs = (s0_scr, s1_scr)

    def body(t, carry):
        for kb in range(per_trip):
            blk = per_trip * t + kb
            bufs[(kb + 1) % 2][...] = scores(0, tile, blk + 1)
            absorb(bufs[kb % 2], 0, 0, tile, values(blk))
        return carry

    lax.fori_loop(0, i * (nb // per_trip), body, 0)

    for kb in range(nb):
        cur, nxt = bufs[kb % 2], bufs[(kb + 1) % 2]
        q0 = kb * block
        if kb + 1 < nb:
            nxt[:, pl.ds(0, tile - q0 - block)] = scores(q0 + block, tile - q0 - block, nb * i + kb + 1)
        vt = values(nb * i + kb)
        absorb(cur, 0, q0, block, vt, key0=q0)
        finish(q0, block)
        if kb + 1 < nb:
            absorb(cur, block, q0 + block, tile - q0 - block, vt)


def _attention(qt, k, vt, batch, seq):
    tq = ATTN_TILE
    nq = seq // tq
    stat = pltpu.VMEM((1, tq), _f32)
    score = pltpu.VMEM((ATTN_BLOCK, tq), _f32)
    return pl.pallas_call(
        functools.partial(_attn_kernel, tile=ATTN_TILE, block=ATTN_BLOCK, chunk=ATTN_CHUNK,
                          per_trip=ATTN_BLOCKS_PER_TRIP),
        grid=(batch, N_HEADS, nq),
        in_specs=[pl.BlockSpec((QK_PAD, tq), lambda b, h, i: (h, b * nq + i)),
                  pl.BlockSpec((seq, QK_PAD), lambda b, h, i: (b, h)),
                  pl.BlockSpec((V_HEAD, seq), lambda b, h, i: (h, b))],
        out_specs=pl.BlockSpec((tq, V_HEAD), lambda b, h, i: (b * nq + i, h)),
        out_shape=jax.ShapeDtypeStruct((batch * seq, N_HEADS * V_HEAD), _bf16),
        scratch_shapes=[score, score, stat, pltpu.VMEM((V_HEAD + SUM_ROWS, tq), _f32)],
        compiler_params=_params(("arbitrary", "arbitrary", "arbitrary")),
        name="attention",
    )(qt, k, vt)


def _back_kernel(x_ref, attn_ref, p_ref, gm_ref, w_u_ref, w_pool_ref, ps_ref, w_gates_ref, w_a_ref, w_b_ref,
                 w_o_ref, gf_ref, w_gate_ref, w_up_ref, w_down_ref, gp_ref, w_pg_ref, w_ple_ref, gl_ref,
                 o_ref, carry_ref, *, final, tm, seq):
    t0 = (pl.program_id(0) * tm) % seq

    @pl.when(t0 == 0)
    def _():
        carry_ref[...] = jnp.zeros_like(carry_ref)

    x = x_ref[...]
    hm = _rms(x, gm_ref[...]).astype(_bf16)
    mixed = _pool_mixer(_dot(hm, w_u_ref[...]), carry_ref, w_pool_ref, ps_ref, t0)
    gate_a = jax.nn.sigmoid(_dot(hm, w_gates_ref[:, :D_MODEL]))
    gate_b = jax.nn.sigmoid(_dot(hm, w_gates_ref[:, D_MODEL:]))
    y_a = _dot(mixed, w_a_ref[...])
    y_b = _dot(attn_ref[...], w_b_ref[...])
    merged = gate_a * y_a + gate_b * y_b
    x = x + _dot(merged.astype(_bf16), w_o_ref[...])
    h = _rms(x, gf_ref[...]).astype(_bf16)
    acc = None
    for c0 in range(0, D_FF, FF_CHUNK):
        cols = slice(c0, min(c0 + FF_CHUNK, D_FF))
        gate = _dot(h, w_gate_ref[:, cols])
        up = _dot(h, w_up_ref[:, cols])
        act = (gate * jax.nn.sigmoid(gate) * up).astype(_bf16)
        part = _dot(act, w_down_ref[cols, :])
        acc = part if acc is None else acc + part
    x = x + acc
    hp = _rms(x, gp_ref[...]).astype(_bf16)
    gate = jax.nn.sigmoid(_dot(hp, w_pg_ref[...]))
    x = x + gate * _dot(p_ref[...].astype(_bf16), w_ple_ref[...])
    if final:
        x = _rms(x, gl_ref[...])
    o_ref[...] = x


def _back(x2, attn, p3, layer, g_mix, w_front, weights, gl, final, seq):
    t = x2.shape[0]
    tm = BACK_TILE

    def row(width):
        return pl.BlockSpec((tm, width), lambda i: (i, 0))

    w_u_spec = pl.BlockSpec((None, D_MODEL, POOL_WIDTH), lambda i: (layer, 0, 0), pipeline_mode=pl.Buffered(1))
    return pl.pallas_call(
        functools.partial(_back_kernel, final=final, tm=tm, seq=seq),
        grid=(t // tm,),
        in_specs=[row(D_MODEL), row(D_MODEL), pl.BlockSpec((None, tm, PLE_DIM), lambda i: (layer, i, 0)),
                  _layer(g_mix, layer), w_u_spec]
                 + [_layer(w, layer) for w in weights] + [_resident(gl.shape)],
        out_specs=row(D_MODEL),
        out_shape=jax.ShapeDtypeStruct((t, D_MODEL), _f32),
        scratch_shapes=[pltpu.VMEM((POOL_HALO, POOL_WIDTH), _f32)],
        compiler_params=_params(("arbitrary",)),
        name="back",
    )(x2, attn, p3, g_mix, w_front, *weights, gl)


def _w_in_layout_kernel(wt_ref, front_ref, gates_ref):
    head = wt_ref[:OFF_KR, :].T
    b = wt_ref[OFF_KR:OFF_KR + LANES, :].T
    tail = wt_ref[OFF_KR + QK_ROPE:, :].T
    lane = lax.broadcasted_iota(jnp.int32, b.shape, 1)
    half = QK_ROPE // 2
    kr = jnp.where(lane < QK_ROPE, b,
                   jnp.where(lane < QK_ROPE + half, -pltpu.roll(b, half, axis=1),
                             pltpu.roll(b, QK_ROPE + half, axis=1)))
    front_ref[...] = jnp.concatenate([head, kr], axis=1).astype(_bf16)
    gates_ref[...] = tail.astype(_bf16)


def _prep_w_in(w_in):
    depth, rows, width = w_in.shape
    tr = W_IN_LAYOUT_ROWS
    return pl.pallas_call(
        _w_in_layout_kernel,
        grid=(depth, rows // tr),
        in_specs=[pl.BlockSpec((None, width, tr), lambda l, r: (l, 0, r))],
        out_specs=[pl.BlockSpec((None, tr, FRONT_WIDTH), lambda l, r: (l, r, 0)),
                   pl.BlockSpec((None, tr, GATE_WIDTH), lambda l, r: (l, r, 0))],
        out_shape=[jax.ShapeDtypeStruct((depth, rows, FRONT_WIDTH), _bf16),
                   jax.ShapeDtypeStruct((depth, rows, GATE_WIDTH), _bf16)],
        compiler_params=_params(("arbitrary", "arbitrary")),
        name="w_in_layout",
    )(jnp.swapaxes(w_in, 1, 2))


def _prep_w_uq_t(w_uq):
    scale = QK_HEAD ** -0.5 * LOG2E
    w = (w_uq * scale).astype(_bf16)
    return jnp.swapaxes(w.reshape(w.shape[0], Q_LORA, N_HEADS * QK_HEAD), 1, 2)


def kernel(x, p, positions, norm_mix, w_in, w_pool, pool_scale, q_norm, kv_norm, w_uq, w_ukv, w_a, w_b, w_o,
           norm_ffn, w_gate, w_up, w_down, norm_ple, w_ple_gate, w_ple, final_norm):
    batch, seq, _ = x.shape
    depth = w_in.shape[0]
    t = batch * seq
    assert t % FRONT_TILE == 0 and seq % BACK_TILE == 0 and seq % ATTN_TILE == 0
    assert ATTN_TILE % (ATTN_BLOCKS_PER_TRIP * ATTN_BLOCK) == 0 and ATTN_BLOCK % ATTN_CHUNK == 0
    assert t % ROPE_TILE == 0 and N_HEADS % Q_HEAD_GROUP == 0

    inv_freq = 1.0 / (ROPE_THETA ** (jnp.arange(0, QK_ROPE, 2, dtype=_f32) / QK_ROPE))
    tables = _rope_tables(positions.astype(_f32).reshape(1, t), inv_freq[:, None])

    def vec(a):
        return a[:, None, :]

    def cast(a):
        return a.astype(_bf16)

    g_mix, g_q, g_kv, g_ffn, g_ple = vec(norm_mix), vec(q_norm), vec(kv_norm), vec(norm_ffn), vec(norm_ple)
    ps = vec(pool_scale)
    w_front, w_gates = _prep_w_in(w_in)
    w_pool_b = cast(w_pool)
    w_uqt = _prep_w_uq_t(w_uq)
    w_ukv_b = cast(w_ukv)
    w_uk = w_ukv_b[..., :QK_NOPE].reshape(depth, KV_LORA, N_HEADS * QK_NOPE)
    w_vt = jnp.swapaxes(w_ukv_b[..., QK_NOPE:].reshape(depth, KV_LORA, N_HEADS * V_HEAD), 1, 2)
    w_a_b, w_b_b, w_o_b = cast(w_a), cast(w_b), cast(w_o)
    w_gate_b, w_up_b, w_down_b = cast(w_gate), cast(w_up), cast(w_down)
    w_pg_b, w_ple_b = cast(w_ple_gate), cast(w_ple)
    gl = final_norm.reshape(1, D_MODEL)
    p3 = p.reshape(depth, t, PLE_DIM)

    x2 = x.reshape(t, D_MODEL)
    for i in range(depth):
        qt, k, vt = _front(x2, i, g_mix, w_front, g_q, g_kv, w_uqt, w_uk, w_vt, tables)
        attn = _attention(qt, k, vt, batch, seq)
        back_weights = (w_pool_b, ps, w_gates, w_a_b, w_b_b, w_o_b, g_ffn, w_gate_b, w_up_b, w_down_b, g_ple,
                        w_pg_b, w_ple_b)
        x2 = _back(x2, attn, p3, i, g_mix, w_front, back_weights, gl, final=(i == depth - 1), seq=seq)
    return x2.reshape(batch, seq, D_MODEL)
```

```python
import functools

import jax
import jax.numpy as jnp
from jax import lax
from jax.experimental import pallas as pl
from jax.experimental.pallas import tpu as pltpu

D_MODEL = 1024
PLE_DIM = 256
POOL_WINDOWS = (2, 4, 8, 16)
POOL_GROUP = 128
POOL_WIDTH = 512
N_HEADS = 8
Q_LORA = 512
KV_LORA = 256
QK_NOPE = 128
QK_ROPE = 64
QK_HEAD = QK_NOPE + QK_ROPE
V_HEAD = 128
D_FF = 2816
ROPE_THETA = 10000.0
EPS = 1e-6

LANES = 128
QK_PAD = 2 * LANES
POOL_HALO = 16

OFF_U = 0
OFF_CQ = OFF_U + POOL_WIDTH
OFF_CKV = OFF_CQ + Q_LORA
OFF_KR = OFF_CKV + KV_LORA
FRONT_WIDTH = OFF_KR + LANES
GATE_WIDTH = 2 * D_MODEL

FRONT_TILE = 1024
BACK_TILE = 512
ROPE_TILE = 2048
W_IN_LAYOUT_ROWS = 256
Q_HEAD_GROUP = 4
ATTN_TILE = 4096
ATTN_BLOCK = 512
ATTN_BLOCKS_PER_TRIP = 4
ATTN_CHUNK = 256
SUM_ROWS = 16
FF_CHUNK = 256
VMEM_LIMIT = 60 * 1024 * 1024

MASK_VALUE = -1e30
LOG2E = 1.4426950408889634

_f32 = jnp.float32
_bf16 = jnp.bfloat16
_NT = (((1,), (1,)), ((), ()))


def _dot(a, b):
    return jnp.dot(a, b, preferred_element_type=_f32)


def _dot_nt(a, b):
    return lax.dot_general(a, b, _NT, preferred_element_type=_f32)


def _rms(x, g):
    return x * lax.rsqrt(jnp.mean(x * x, axis=-1, keepdims=True) + EPS) * g


def _resident(shape):
    nd = len(shape)
    return pl.BlockSpec(shape, lambda *_: (0,) * nd, pipeline_mode=pl.Buffered(1))


def _layer(arr, layer):
    nd = arr.ndim - 1
    return pl.BlockSpec((None,) + arr.shape[1:], lambda *_: (layer,) + (0,) * nd, pipeline_mode=pl.Buffered(1))


def _params(sem):
    return pltpu.CompilerParams(dimension_semantics=sem, vmem_limit_bytes=VMEM_LIMIT)


def _rope_table_kernel(pos_ref, inv_ref, cos_ref, sin_ref, cost_ref, sint_ref):
    ang = inv_ref[...] * pos_ref[...]
    c = jnp.cos(ang)
    s = jnp.sin(ang)
    zeros = jnp.zeros((LANES - QK_ROPE, ang.shape[1]), _f32)
    ct = jnp.concatenate([c, c], axis=0)
    st = jnp.concatenate([s, s], axis=0)
    cost_ref[...] = ct
    sint_ref[...] = st
    cos_ref[...] = jnp.concatenate([ct, zeros], axis=0).T
    sin_ref[...] = jnp.concatenate([st, zeros], axis=0).T


def _rope_tables(pos_row, inv_col):
    t = pos_row.shape[1]
    tm = ROPE_TILE
    row = pl.BlockSpec((tm, LANES), lambda i: (i, 0))
    col = pl.BlockSpec((QK_ROPE, tm), lambda i: (0, i))
    return pl.pallas_call(
        _rope_table_kernel,
        grid=(t // tm,),
        in_specs=[pl.BlockSpec((1, tm), lambda i: (0, i)), _resident(inv_col.shape)],
        out_specs=[row, row, col, col],
        out_shape=[jax.ShapeDtypeStruct((t, LANES), _f32)] * 2 + [jax.ShapeDtypeStruct((QK_ROPE, t), _f32)] * 2,
        compiler_params=_params(("arbitrary",)),
        name="rope_tables",
    )(pos_row, inv_col)


def _front_kernel(x_ref, g_ref, w_in_ref, w_pool_ref, ps_ref, qg_ref, kvg_ref, w_uqt_ref, w_uk_ref, w_vt_ref,
                  cos_ref, sin_ref, cost_ref, sint_ref,
                  mixed_ref, qt_ref, kn_ref, kpe_ref, vt_ref,
                  carry_ref, *, tm, seq):
    i = pl.program_id(0)
    t0 = (i * tm) % seq

    @pl.when(t0 == 0)
    def _():
        carry_ref[...] = jnp.zeros_like(carry_ref)

    h = _rms(x_ref[...], g_ref[...]).astype(_bf16)

    u = _dot(h, w_in_ref[:, OFF_U:OFF_U + POOL_WIDTH])
    ext = jnp.concatenate([carry_ref[...], u], axis=0)
    carry_ref[...] = u[tm - POOL_HALO:, :]
    sums = []
    cur = ext
    step = 1
    for _ in POOL_WINDOWS:
        cur = cur + pltpu.roll(cur, step, axis=0)
        sums.append(cur[POOL_HALO:, :POOL_GROUP])
        cur = cur[:, POOL_GROUP:]
        step *= 2
    t_idx = t0 + lax.broadcasted_iota(jnp.int32, (tm, POOL_GROUP), 0)
    for g, w in enumerate(POOL_WINDOWS):
        cols = slice(g * POOL_GROUP, (g + 1) * POOL_GROUP)
        cnt = jnp.minimum(t_idx + 1, w).astype(_f32)
        pooled = sums[g] / cnt - u[:, cols]
        mixed = _dot(pooled.astype(_bf16), w_pool_ref[g]) * ps_ref[:, cols]
        mixed_ref[:, cols] = mixed.astype(_bf16)

    cq = _dot(h, w_in_ref[:, OFF_CQ:OFF_CQ + Q_LORA])
    cqn = _rms(cq, qg_ref[...]).astype(_bf16)
    half = QK_ROPE // 2
    cos_h = cost_ref[:half, :]
    sin_h = sint_ref[:half, :]
    rows_per_group = Q_HEAD_GROUP * QK_HEAD
    for g0 in range(0, N_HEADS, Q_HEAD_GROUP):
        qt = _dot_nt(w_uqt_ref[g0 * QK_HEAD:g0 * QK_HEAD + rows_per_group, :], cqn)
        for j in range(Q_HEAD_GROUP):
            r = j * QK_HEAD
            o = (g0 + j) * QK_PAD
            x1 = qt[r + QK_NOPE:r + QK_NOPE + half]
            x2 = qt[r + QK_NOPE + half:r + QK_HEAD]
            qt_ref[o:o + QK_NOPE, :] = qt[r:r + QK_NOPE].astype(_bf16)
            qt_ref[o + QK_NOPE:o + QK_NOPE + half, :] = (x1 * cos_h - x2 * sin_h).astype(_bf16)
            qt_ref[o + QK_NOPE + half:o + QK_HEAD, :] = (x2 * cos_h + x1 * sin_h).astype(_bf16)
            qt_ref[o + QK_HEAD:o + QK_PAD, :] = jnp.zeros((QK_PAD - QK_HEAD, tm), _bf16)

    ckv = _dot(h, w_in_ref[:, OFF_CKV:OFF_CKV + KV_LORA])
    ckvn = _rms(ckv, kvg_ref[...]).astype(_bf16)
    kr = _dot(h, w_in_ref[:, OFF_KR:OFF_KR + LANES])
    k_pe = (kr * cos_ref[...] + pltpu.roll(kr, QK_ROPE, axis=1) * sin_ref[...]).astype(_bf16)
    kn_ref[...] = _dot(ckvn, w_uk_ref[...]).astype(_bf16)
    kpe_ref[...] = k_pe
    vt_ref[...] = _dot_nt(w_vt_ref[...], ckvn).astype(_bf16)


def _front(x2, layer, g, w_in, w_pool, ps, qg, kvg, w_uqt, w_uk, w_vt, tables, seq):
    t = x2.shape[0]
    tm = FRONT_TILE
    cos_t, sin_t, cos_tt, sin_tt = tables

    def row(width):
        return pl.BlockSpec((tm, width), lambda i: (i, 0))

    def col(height):
        return pl.BlockSpec((height, tm), lambda i: (0, i))

    def row_out(width):
        return jax.ShapeDtypeStruct((t, width), _bf16)

    def col_out(height):
        return jax.ShapeDtypeStruct((height, t), _bf16)

    weights = (g, w_in, w_pool, ps, qg, kvg, w_uqt, w_uk, w_vt)
    return pl.pallas_call(
        functools.partial(_front_kernel, tm=tm, seq=seq),
        grid=(t // tm,),
        in_specs=[row(D_MODEL)] + [_layer(w, layer) for w in weights]
                 + [row(LANES), row(LANES), col(QK_ROPE), col(QK_ROPE)],
        out_specs=[row(POOL_WIDTH), col(N_HEADS * QK_PAD), row(N_HEADS * QK_NOPE), row(LANES),
                   col(N_HEADS * V_HEAD)],
        out_shape=[row_out(POOL_WIDTH), col_out(N_HEADS * QK_PAD), row_out(N_HEADS * QK_NOPE), row_out(LANES),
                   col_out(N_HEADS * V_HEAD)],
        scratch_shapes=[pltpu.VMEM((POOL_HALO, POOL_WIDTH), _f32)],
        compiler_params=_params(("arbitrary",)),
        name="front",
    )(x2, *weights, cos_t, sin_t, cos_tt, sin_tt)


def _attn_kernel(qt_ref, kn_ref, kpe_ref, vt_ref, o_ref, s0_scr, s1_scr, m_scr, acc_scr, *, tile, block,
                 chunk, per_trip):
    i = pl.program_id(2)
    nb = tile // block

    def scores(q_col0, nq, blk):
        keys = pl.ds(pl.multiple_of(blk * block, block), block)
        k = jnp.concatenate([kn_ref[keys, :], kpe_ref[keys, :]], axis=1)
        return _dot(k, qt_ref[:, pl.ds(q_col0, nq)])

    def values(blk):
        vt = vt_ref[:, pl.ds(pl.multiple_of(blk * block, block), block)]
        return jnp.concatenate([vt, jnp.ones((SUM_ROWS, block), _bf16)], axis=0)

    def absorb(s_ref, s_col0, col0, ncols, vt, key0=None):
        for c in range(ncols // chunk):
            cols = pl.ds(col0 + c * chunk, chunk)
            if key0 is None:
                s = s_ref[:, pl.ds(s_col0 + c * chunk, chunk)]
                vc = vt
            else:
                nkeys = col0 + (c + 1) * chunk - key0
                s = s_ref[pl.ds(0, nkeys), pl.ds(s_col0 + c * chunk, chunk)]
                vc = vt[:, :nkeys]
                key_id = key0 + lax.broadcasted_iota(jnp.int32, s.shape, 0)
                q_id = col0 + c * chunk + lax.broadcasted_iota(jnp.int32, s.shape, 1)
                s = jnp.where(key_id <= q_id, s, MASK_VALUE)
            m_old = m_scr[:, cols]
            m_new = jnp.maximum(m_old, jnp.max(s, axis=0, keepdims=True))
            alpha = jnp.exp2(m_old - m_new)
            p = jnp.exp2(s - m_new)
            m_scr[:, cols] = m_new
            acc_scr[:, cols] = alpha * acc_scr[:, cols] + _dot(vc, p.astype(_bf16))

    def finish(col0, ncols):
        a = acc_scr[:, pl.ds(col0, ncols)]
        o_ref[pl.ds(col0, ncols), :] = (a[:V_HEAD] / a[V_HEAD:V_HEAD + 1]).T.astype(_bf16)

    m_scr[...] = jnp.full(m_scr.shape, MASK_VALUE, _f32)
    acc_scr[...] = jnp.zeros(acc_scr.shape, _f32)
    s0_scr[...] = scores(0, tile, 0)

    bufs = (s0_scr, s1_scr)

    def body(t, carry):
        for kb in range(per_trip):
            blk = per_trip * t + kb
            bufs[(kb + 1) % 2][...] = scores(0, tile, blk + 1)
            absorb(bufs[kb % 2], 0, 0, tile, values(blk))
        return carry

    lax.fori_loop(0, i * (nb // per_trip), body, 0)

    for kb in range(nb):
        cur, nxt = bufs[kb % 2], bufs[(kb + 1) % 2]
        q0 = kb * block
        if kb + 1 < nb:
            nxt[:, pl.ds(0, tile - q0 - block)] = scores(q0 + block, tile - q0 - block, nb * i + kb + 1)
        vt = values(nb * i + kb)
        absorb(cur, 0, q0, block, vt, key0=q0)
        finish(q0, block)
        if kb + 1 < nb:
            absorb(cur, block, q0 + block, tile - q0 - block, vt)


def _attention(qt, kn, kpe, vt, batch, seq):
    tq = ATTN_TILE
    nq = seq // tq
    stat = pltpu.VMEM((1, tq), _f32)
    score = pltpu.VMEM((ATTN_BLOCK, tq), _f32)
    return pl.pallas_call(
        functools.partial(_attn_kernel, tile=ATTN_TILE, block=ATTN_BLOCK, chunk=ATTN_CHUNK,
                          per_trip=ATTN_BLOCKS_PER_TRIP),
        grid=(batch, N_HEADS, nq),
        in_specs=[pl.BlockSpec((QK_PAD, tq), lambda b, h, i: (h, b * nq + i)),
                  pl.BlockSpec((seq, QK_NOPE), lambda b, h, i: (b, h)),
                  pl.BlockSpec((seq, LANES), lambda b, h, i: (b, 0)),
                  pl.BlockSpec((V_HEAD, seq), lambda b, h, i: (h, b))],
        out_specs=pl.BlockSpec((tq, V_HEAD), lambda b, h, i: (b * nq + i, h)),
        out_shape=jax.ShapeDtypeStruct((batch * seq, N_HEADS * V_HEAD), _bf16),
        scratch_shapes=[score, score, stat, pltpu.VMEM((V_HEAD + SUM_ROWS, tq), _f32)],
        compiler_params=_params(("arbitrary", "arbitrary", "arbitrary")),
        name="attention",
    )(qt, kn, kpe, vt)


def _back_kernel(x_ref, attn_ref, mixed_ref, p_ref, gm_ref, w_gates_ref, w_a_ref, w_b_ref, w_o_ref,
                 gf_ref, w_gate_ref, w_up_ref, w_down_ref, gp_ref, w_pg_ref, w_ple_ref, gl_ref,
                 o_ref, *, final):
    x = x_ref[...]
    hm = _rms(x, gm_ref[...]).astype(_bf16)
    gate_a = jax.nn.sigmoid(_dot(hm, w_gates_ref[:, :D_MODEL]))
    gate_b = jax.nn.sigmoid(_dot(hm, w_gates_ref[:, D_MODEL:]))
    y_a = _dot(mixed_ref[...], w_a_ref[...])
    y_b = _dot(attn_ref[...], w_b_ref[...])
    merged = gate_a * y_a + gate_b * y_b
    x = x + _dot(merged.astype(_bf16), w_o_ref[...])
    h = _rms(x, gf_ref[...]).astype(_bf16)
    acc = None
    for c0 in range(0, D_FF, FF_CHUNK):
        cols = slice(c0, min(c0 + FF_CHUNK, D_FF))
        gate = _dot(h, w_gate_ref[:, cols])
        up = _dot(h, w_up_ref[:, cols])
        act = (gate * jax.nn.sigmoid(gate) * up).astype(_bf16)
        part = _dot(act, w_down_ref[cols, :])
        acc = part if acc is None else acc + part
    x = x + acc
    hp = _rms(x, gp_ref[...]).astype(_bf16)
    gate = jax.nn.sigmoid(_dot(hp, w_pg_ref[...]))
    x = x + gate * _dot(p_ref[...].astype(_bf16), w_ple_ref[...])
    if final:
        x = _rms(x, gl_ref[...])
    o_ref[...] = x


def _back(x2, attn, mixed, p3, layer, weights, gl, final):
    t = x2.shape[0]
    tm = BACK_TILE

    def row(width):
        return pl.BlockSpec((tm, width), lambda i: (i, 0))

    return pl.pallas_call(
        functools.partial(_back_kernel, final=final),
        grid=(t // tm,),
        in_specs=[row(D_MODEL), row(D_MODEL), row(POOL_WIDTH),
                  pl.BlockSpec((None, tm, PLE_DIM), lambda i: (layer, i, 0))]
                 + [_layer(w, layer) for w in weights] + [_resident(gl.shape)],
        out_specs=row(D_MODEL),
        out_shape=jax.ShapeDtypeStruct((t, D_MODEL), _f32),
        compiler_params=_params(("arbitrary",)),
        name="back",
    )(x2, attn, mixed, p3, *weights, gl)


def _w_in_layout_kernel(wt_ref, front_ref, gates_ref):
    head = wt_ref[:OFF_KR, :].T
    b = wt_ref[OFF_KR:OFF_KR + LANES, :].T
    tail = wt_ref[OFF_KR + QK_ROPE:, :].T
    lane = lax.broadcasted_iota(jnp.int32, b.shape, 1)
    half = QK_ROPE // 2
    kr = jnp.where(lane < QK_ROPE, b,
                   jnp.where(lane < QK_ROPE + half, -pltpu.roll(b, half, axis=1),
                             pltpu.roll(b, QK_ROPE + half, axis=1)))
    front_ref[...] = jnp.concatenate([head, kr], axis=1).astype(_bf16)
    gates_ref[...] = tail.astype(_bf16)


def _prep_w_in(w_in):
    depth, rows, width = w_in.shape
    tr = W_IN_LAYOUT_ROWS
    return pl.pallas_call(
        _w_in_layout_kernel,
        grid=(depth, rows // tr),
        in_specs=[pl.BlockSpec((None, width, tr), lambda l, r: (l, 0, r))],
        out_specs=[pl.BlockSpec((None, tr, FRONT_WIDTH), lambda l, r: (l, r, 0)),
                   pl.BlockSpec((None, tr, GATE_WIDTH), lambda l, r: (l, r, 0))],
        out_shape=[jax.ShapeDtypeStruct((depth, rows, FRONT_WIDTH), _bf16),
                   jax.ShapeDtypeStruct((depth, rows, GATE_WIDTH), _bf16)],
        compiler_params=_params(("arbitrary", "arbitrary")),
        name="w_in_layout",
    )(jnp.swapaxes(w_in, 1, 2))


def _prep_w_uq_t(w_uq):
    scale = QK_HEAD ** -0.5 * LOG2E
    w = (w_uq * scale).astype(_bf16)
    return jnp.swapaxes(w.reshape(w.shape[0], Q_LORA, N_HEADS * QK_HEAD), 1, 2)


def kernel(x, p, positions, norm_mix, w_in, w_pool, pool_scale, q_norm, kv_norm, w_uq, w_ukv, w_a, w_b, w_o,
           norm_ffn, w_gate, w_up, w_down, norm_ple, w_ple_gate, w_ple, final_norm):
    batch, seq, _ = x.shape
    depth = w_in.shape[0]
    t = batch * seq
    assert seq % FRONT_TILE == 0 and t % BACK_TILE == 0 and seq % ATTN_TILE == 0
    assert ATTN_TILE % (ATTN_BLOCKS_PER_TRIP * ATTN_BLOCK) == 0 and ATTN_BLOCK % ATTN_CHUNK == 0
    assert t % ROPE_TILE == 0 and N_HEADS % Q_HEAD_GROUP == 0

    inv_freq = 1.0 / (ROPE_THETA ** (jnp.arange(0, QK_ROPE, 2, dtype=_f32) / QK_ROPE))
    tables = _rope_tables(positions.astype(_f32).reshape(1, t), inv_freq[:, None])

    def vec(a):
        return a[:, None, :]

    def cast(a):
        return a.astype(_bf16)

    g_mix, g_q, g_kv, g_ffn, g_ple = vec(norm_mix), vec(q_norm), vec(kv_norm), vec(norm_ffn), vec(norm_ple)
    ps = vec(pool_scale)
    w_front, w_gates = _prep_w_in(w_in)
    w_pool_b = cast(w_pool)
    w_uqt = _prep_w_uq_t(w_uq)
    w_ukv_b = cast(w_ukv)
    w_uk = w_ukv_b[..., :QK_NOPE].reshape(depth, KV_LORA, N_HEADS * QK_NOPE)
    w_vt = jnp.swapaxes(w_ukv_b[..., QK_NOPE:].reshape(depth, KV_LORA, N_HEADS * V_HEAD), 1, 2)
    w_a_b, w_b_b, w_o_b = cast(w_a), cast(w_b), cast(w_o)
    w_gate_b, w_up_b, w_down_b = cast(w_gate), cast(w_up), cast(w_down)
    w_pg_b, w_ple_b = cast(w_ple_gate), cast(w_ple)
    gl = final_norm.reshape(1, D_MODEL)
    p3 = p.reshape(depth, t, PLE_DIM)

    x2 = x.reshape(t, D_MODEL)
    for i in range(depth):
        mixed, qt, kn, kpe, vt = _front(x2, i, g_mix, w_front, w_pool_b, ps, g_q, g_kv, w_uqt, w_uk, w_vt, tables,
                                        seq)
        attn = _attention(qt, kn, kpe, vt, batch, seq)
        back_weights = (g_mix, w_gates, w_a_b, w_b_b, w_o_b, g_ffn, w_gate_b, w_up_b, w_down_b, g_ple, w_pg_b,
                        w_ple_b)
        x2 = _back(x2, attn, mixed, p3, i, back_weights, gl, final=(i == depth - 1))
    return x2.reshape(batch, seq, D_MODEL)
```
